```python
import math
import jax
import jax.numpy as jnp
from jax import lax
import numpy as np

D_MODEL = 4096
BATCH = 2
SEQ = 8192
DEPTH = 2

GRID_W = 64
CTX_LEN = 256
HEAD_DIM = 128
ROPE_THETA = 10000.0
Q_BLOCK = 128
EPS = 1e-6
A_HEADS = D_MODEL // (2 * HEAD_DIM)
A_KV_HEADS = A_HEADS // 4
B_HEADS = D_MODEL // (4 * HEAD_DIM)
B_VDIM = 2 * HEAD_DIM
C_HEADS = D_MODEL // HEAD_DIM
NA_ROWS = 8
NA_COLS = 16
MLP_HIDDEN = 4 * D_MODEL
N_EVEN = (DEPTH + 1) // 2
N_ODD = DEPTH // 2
A_Q = A_HEADS * HEAD_DIM
A_KV = A_KV_HEADS * HEAD_DIM
B_QK = B_HEADS * 2 * HEAD_DIM
B_V = B_HEADS * B_VDIM
EVEN_IN = A_Q + 2 * A_KV + 2 * B_QK + B_V
EVEN_MIX = A_Q + B_V
ODD_MIX = C_HEADS * HEAD_DIM
ODD_IN = 3 * ODD_MIX

kernel_name = 'hybrid_dit_gqa_diffattn_natten_block'


def rms_norm(x, g):
    xf = x.astype(jnp.float32)
    y = xf * lax.rsqrt(jnp.mean(xf * xf, axis=-1, keepdims=True) + EPS)
    return (y * g.astype(jnp.float32)).astype(x.dtype)


def modulate(x, g, shift, scale):
    return rms_norm(x, g) * (1 + scale) + shift


def axial_rope_tables(n_tokens, dtype):
    t = jnp.arange(n_tokens, dtype=jnp.int32)
    row = (t // GRID_W).astype(jnp.float32)
    col = (t % GRID_W).astype(jnp.float32)
    n_freq = HEAD_DIM // 4
    freqs = ROPE_THETA ** (-jnp.arange(n_freq, dtype=jnp.float32) / n_freq)
    ang = jnp.stack([row[:, None] * freqs, col[:, None] * freqs], axis=1)
    return jnp.cos(ang).astype(dtype), jnp.sin(ang).astype(dtype)


def apply_rope(x, cos, sin):
    b, l, h, d = x.shape
    xs = x.reshape(b, l, h, 2, 2, d // 4)
    x0, x1 = xs[..., 0, :], xs[..., 1, :]
    cs, sn = cos[None, :, None], sin[None, :, None]
    out = jnp.stack([x0 * cs - x1 * sn, x1 * cs + x0 * sn], axis=-2)
    return out.reshape(b, l, h, d)


def gqa_blocks(q, k, v):
    b, l, hq, d = q.shape
    hkv = k.shape[2]
    grp = hq // hkv
    nb = l // Q_BLOCK
    qb = q.reshape(b, nb, Q_BLOCK, hkv, grp, d).swapaxes(0, 1)
    scale = d ** -0.5

    def one(qblk):
        s = jnp.einsum('bqhgd,bnhd->bhgqn', qblk, k).astype(jnp.float32) * scale
        p = jax.nn.softmax(s, axis=-1).astype(v.dtype)
        return jnp.einsum('bhgqn,bnhd->bqhgd', p, v)

    o = lax.map(one, qb)
    return o.swapaxes(0, 1).reshape(b, l, hq * d)


def diff_blocks(q, k, v, lam, subln_g, lambda_init):
    b, l, h, _, d = q.shape
    nb = l // Q_BLOCK
    qb = q.reshape(b, nb, Q_BLOCK, h, 2, d).swapaxes(0, 1)
    scale = d ** -0.5

    def one(qblk):
        s = jnp.einsum('bqhmd,bnhmd->bhmqn', qblk, k).astype(jnp.float32) * scale
        p = jax.nn.softmax(s, axis=-1)
        a = (p[:, :, 0] - lam * p[:, :, 1]).astype(v.dtype)
        return jnp.einsum('bhqn,bnhe->bqhe', a, v)

    o = lax.map(one, qb).swapaxes(0, 1).reshape(b, l, h, v.shape[-1])
    o = rms_norm(o, subln_g) * (1.0 - lambda_init)
    return o.reshape(b, l, h * v.shape[-1])


def na_latent(q, k, v, kc, vc, rel_bias, rows):
    b, s, h, d = q.shape
    kr = min(NA_ROWS, rows)
    scale = d ** -0.5
    qg = q.reshape(b, rows, GRID_W, h, d)
    kg = k.reshape(b, rows, GRID_W, h, d)
    vg = v.reshape(b, rows, GRID_W, h, d)
    col = jnp.arange(GRID_W, dtype=jnp.int32)
    c0 = jnp.clip(col - NA_COLS // 2, 0, GRID_W - NA_COLS)
    col_idx = c0[:, None] + jnp.arange(NA_COLS, dtype=jnp.int32)[None]
    dc = col_idx - col[:, None] + (NA_COLS - 1)
    r_idx = jnp.arange(rows, dtype=jnp.int32)
    r0 = jnp.clip(r_idx - kr // 2, 0, rows - kr)

    def one(args):
        qrow, r, r0r = args
        kband = lax.dynamic_slice_in_dim(kg, r0r, kr, axis=1)
        vband = lax.dynamic_slice_in_dim(vg, r0r, kr, axis=1)
        kwin = kband[:, :, col_idx]
        vwin = vband[:, :, col_idx]
        dr = r0r + jnp.arange(kr, dtype=jnp.int32) - r + (NA_ROWS - 1)
        bias = rel_bias[:, dr[:, None, None], dc[None]]
        s_loc = jnp.einsum('bqhd,brqjhd->bhqrj', qrow, kwin).astype(jnp.float32) * scale
        s_loc = s_loc + bias.transpose(0, 2, 1, 3)[None].astype(jnp.float32)
        s_ctx = jnp.einsum('bqhd,bnhd->bhqn', qrow, kc).astype(jnp.float32) * scale
        n_loc = kr * NA_COLS
        sc = jnp.concatenate([s_loc.reshape(b, h, GRID_W, n_loc), s_ctx], axis=-1)
        p = jax.nn.softmax(sc, axis=-1).astype(v.dtype)
        p_loc = p[..., :n_loc].reshape(b, h, GRID_W, kr, NA_COLS)
        p_ctx = p[..., n_loc:]
        return (jnp.einsum('bhqrj,brqjhd->bqhd', p_loc, vwin)
                + jnp.einsum('bhqn,bnhd->bqhd', p_ctx, vc))

    o = lax.map(one, (qg.swapaxes(0, 1), r_idx, r0))
    return o.swapaxes(0, 1).reshape(b, s, h * d)


def even_mixer(u, uc, w_in, w_out, q_g, k_g, lq1, lk1, lq2, lk2, subln_g, cos, sin, lambda_init, need_ctx):
    idx = [A_Q, A_Q + A_KV, A_Q + 2 * A_KV, A_Q + 2 * A_KV + B_QK, A_Q + 2 * A_KV + 2 * B_QK]

    def project(t):
        bt, l, _ = t.shape
        qa, ka, va, qb, kb, vb = jnp.split(t @ w_in, idx, axis=-1)
        qa = rms_norm(qa.reshape(bt, l, A_HEADS, HEAD_DIM), q_g)
        ka = rms_norm(ka.reshape(bt, l, A_KV_HEADS, HEAD_DIM), k_g)
        va = va.reshape(bt, l, A_KV_HEADS, HEAD_DIM)
        qb = qb.reshape(bt, l, 2 * B_HEADS, HEAD_DIM)
        kb = kb.reshape(bt, l, 2 * B_HEADS, HEAD_DIM)
        vb = vb.reshape(bt, l, B_HEADS, B_VDIM)
        return qa, ka, va, qb, kb, vb

    def to_maps(t):
        return t.reshape(t.shape[0], t.shape[1], B_HEADS, 2, HEAD_DIM)

    qa, ka, va, qb, kb, vb = project(u)
    qa_c, ka_c, va_c, qb_c, kb_c, vb_c = project(uc)
    qa, ka, qb, kb = [apply_rope(t, cos, sin) for t in (qa, ka, qb, kb)]
    f32 = jnp.float32
    lam = (jnp.exp(jnp.sum(lq1.astype(f32) * lk1.astype(f32)))
           - jnp.exp(jnp.sum(lq2.astype(f32) * lk2.astype(f32))) + lambda_init)
    a_lat = gqa_blocks(qa, jnp.concatenate([ka, ka_c], axis=1), jnp.concatenate([va, va_c], axis=1))
    b_lat = diff_blocks(to_maps(qb), to_maps(jnp.concatenate([kb, kb_c], axis=1)),
                        jnp.concatenate([vb, vb_c], axis=1), lam, subln_g, lambda_init)
    y = jnp.concatenate([a_lat, b_lat], axis=-1) @ w_out
    yc = None
    if need_ctx:
        a_ctx = gqa_blocks(qa_c, ka_c, va_c)
        b_ctx = diff_blocks(to_maps(qb_c), to_maps(kb_c), vb_c, lam, subln_g, lambda_init)
        yc = jnp.concatenate([a_ctx, b_ctx], axis=-1) @ w_out
    return y, yc


def odd_mixer(u, uc, w_in, w_out, rel_bias, rows, need_ctx):
    def project(t):
        bt, l, _ = t.shape
        q, k, v = jnp.split(t @ w_in, 3, axis=-1)
        return (q.reshape(bt, l, C_HEADS, HEAD_DIM), k.reshape(bt, l, C_HEADS, HEAD_DIM),
                v.reshape(bt, l, C_HEADS, HEAD_DIM))

    q, k, v = project(u)
    qc, kc, vc = project(uc)
    y = na_latent(q, k, v, kc, vc, rel_bias, rows) @ w_out
    yc = None
    if need_ctx:
        yc = gqa_blocks(qc, kc, vc) @ w_out
    return y, yc


def sq_relu_mlp(h, w1, w2):
    return jnp.square(jax.nn.relu(h @ w1)) @ w2


def setup_inputs(seed: int = 0) -> dict:
    key = jax.random.key(seed)
    ks = jax.random.split(key, 24)
    f32 = jnp.float32
    nrm = lambda k, shape, s: jax.random.normal(k, shape, f32) * s
    return {
        'x': nrm(ks[0], (BATCH, SEQ, D_MODEL), 1.0),
        'c': nrm(ks[1], (BATCH, D_MODEL), 1.0),
        'ctx': nrm(ks[2], (BATCH, CTX_LEN, D_MODEL), 1.0),
        'c_ctx': nrm(ks[3], (D_MODEL,), 1.0),
        'ada_w': nrm(ks[4], (DEPTH, D_MODEL, 6 * D_MODEL), 0.5 * D_MODEL ** -0.5),
        'ada_b': nrm(ks[5], (DEPTH, 6 * D_MODEL), 0.02),
        'norm1_g': 1.0 + nrm(ks[6], (DEPTH, D_MODEL), 0.02),
        'norm2_g': 1.0 + nrm(ks[7], (DEPTH, D_MODEL), 0.02),
        'w_in_even': nrm(ks[8], (N_EVEN, D_MODEL, EVEN_IN), D_MODEL ** -0.5),
        'w_out_even': nrm(ks[9], (N_EVEN, EVEN_MIX, D_MODEL), EVEN_MIX ** -0.5),
        'a_q_norm': 1.0 + nrm(ks[10], (N_EVEN, HEAD_DIM), 0.02),
        'a_k_norm': 1.0 + nrm(ks[11], (N_EVEN, HEAD_DIM), 0.02),
        'b_lambda_q1': nrm(ks[12], (N_EVEN, HEAD_DIM), 0.1),
        'b_lambda_k1': nrm(ks[13], (N_EVEN, HEAD_DIM), 0.1),
        'b_lambda_q2': nrm(ks[14], (N_EVEN, HEAD_DIM), 0.1),
        'b_lambda_k2': nrm(ks[15], (N_EVEN, HEAD_DIM), 0.1),
        'b_subln_g': 1.0 + nrm(ks[16], (N_EVEN, B_VDIM), 0.02),
        'w_in_odd': nrm(ks[17], (N_ODD, D_MODEL, ODD_IN), D_MODEL ** -0.5),
        'w_out_odd': nrm(ks[18], (N_ODD, ODD_MIX, D_MODEL), ODD_MIX ** -0.5),
        'na_rel_bias': nrm(ks[19], (N_ODD, C_HEADS, 2 * NA_ROWS - 1, 2 * NA_COLS - 1), 0.2),
        'mlp_w1': nrm(ks[20], (DEPTH, D_MODEL, MLP_HIDDEN), D_MODEL ** -0.5),
        'mlp_w2': nrm(ks[21], (DEPTH, MLP_HIDDEN, D_MODEL), MLP_HIDDEN ** -0.5),
        'final_g': 1.0 + nrm(ks[22], (D_MODEL,), 0.02),
    }


def reference(x, c, ctx, c_ctx, ada_w, ada_b, norm1_g, norm2_g, w_in_even, w_out_even,
              a_q_norm, a_k_norm, b_lambda_q1, b_lambda_k1, b_lambda_q2, b_lambda_k2, b_subln_g,
              w_in_odd, w_out_odd, na_rel_bias, mlp_w1, mlp_w2, final_g):
    s = x.shape[1]
    rows = s // GRID_W
    cos, sin = axial_rope_tables(s, x.dtype)
    cond_lat = jax.nn.silu(c)
    cond_ctx = jax.nn.silu(c_ctx)
    h, hc = x, ctx
    for i in range(DEPTH):
        need_ctx = i < DEPTH - 1
        mod = cond_lat @ ada_w[i] + ada_b[i]
        mod_c = cond_ctx @ ada_w[i] + ada_b[i]
        sh1, sc1, g1, sh2, sc2, g2 = [m[:, None] for m in jnp.split(mod, 6, axis=-1)]
        csh1, csc1, cg1, csh2, csc2, cg2 = jnp.split(mod_c, 6, axis=-1)
        u = modulate(h, norm1_g[i], sh1, sc1)
        uc = modulate(hc, norm1_g[i], csh1, csc1)
        j = i // 2
        if i % 2 == 0:
            lambda_init = 0.8 - 0.6 * math.exp(-0.3 * i)
            y, yc = even_mixer(u, uc, w_in_even[j], w_out_even[j], a_q_norm[j], a_k_norm[j],
                               b_lambda_q1[j], b_lambda_k1[j], b_lambda_q2[j], b_lambda_k2[j],
                               b_subln_g[j], cos, sin, lambda_init, need_ctx)
        else:
            y, yc = odd_mixer(u, uc, w_in_odd[j], w_out_odd[j], na_rel_bias[j], rows, need_ctx)
        h = h + g1 * y
        h = h + g2 * sq_relu_mlp(modulate(h, norm2_g[i], sh2, sc2), mlp_w1[i], mlp_w2[i])
        if need_ctx:
            hc = hc + cg1 * yc
            hc = hc + cg2 * sq_relu_mlp(modulate(hc, norm2_g[i], csh2, csc2), mlp_w1[i], mlp_w2[i])
    return rms_norm(h, final_g)
```

```python
import functools
import math

import jax
import jax.numpy as jnp
from jax import lax
from jax.experimental import pallas as pl
from jax.experimental.pallas import tpu as pltpu

D_MODEL = 4096
BATCH = 2
SEQ = 8192
DEPTH = 2
GRID_W = 64
ROWS = SEQ // GRID_W
CTX_LEN = 256
HEAD_DIM = 128
ROPE_THETA = 10000.0
EPS = 1e-6
A_HEADS = 16
A_KV_HEADS = 4
A_GROUP = A_HEADS // A_KV_HEADS
B_HEADS = 8
C_HEADS = 32
NA_ROWS = 8
NA_COLS = 16
MLP_HIDDEN = 4 * D_MODEL
EVEN_IN = 9216
SCALE = HEAD_DIM ** -0.5

P_QA, P_KA, P_VA, P_QB, P_KB, P_VB = 0, 16, 20, 24, 40, 56
N_PROJ_EVEN = EVEN_IN // HEAD_DIM
R_QA, R_KA, R_QB, R_KB = 0, 16, 20, 36
N_ROT = 52
N_NORMED = 20

LANES = 128
VMEM_LIMIT = 56 * 1024 * 1024
NEG = -1e30

F32 = jnp.float32
BF16 = jnp.bfloat16


def _params(sem, vmem=VMEM_LIMIT):
    return pltpu.CompilerParams(dimension_semantics=sem, vmem_limit_bytes=vmem)


def _ada_kernel(c_ref, w_ref, b_ref, o_ref):
    x = c_ref[...]
    cond = x / (1.0 + jnp.exp(-x))
    o_ref[...] = jnp.dot(cond.astype(BF16), w_ref[...].astype(BF16),
                         preferred_element_type=F32) + b_ref[...]


def ada_modulation(craw, ada_w, ada_b):
    tn = 512
    n6 = 6 * D_MODEL
    return pl.pallas_call(
        _ada_kernel,
        grid=(DEPTH, n6 // tn),
        in_specs=[pl.BlockSpec((8, D_MODEL), lambda i, n: (0, 0)),
                  pl.BlockSpec((None, D_MODEL, tn), lambda i, n: (i, 0, n)),
                  pl.BlockSpec((None, 1, tn), lambda i, n: (i, 0, n))],
        out_specs=pl.BlockSpec((None, 8, tn), lambda i, n: (i, 0, n)),
        out_shape=jax.ShapeDtypeStruct((DEPTH, 8, n6), F32),
        compiler_params=_params(("arbitrary", "arbitrary")),
        name="ada_modulation",
    )(craw, ada_w, ada_b.reshape(DEPTH, 1, n6))


ROW_CHUNK = 16


def _row_chunks(n_rows, fn):
    def body(i, carry):
        fn(pl.ds(pl.multiple_of(i * ROW_CHUNK, ROW_CHUNK), ROW_CHUNK))
        return carry
    lax.fori_loop(0, n_rows // ROW_CHUNK, body, 0)


def _modulate_into(dst_ref, h_ref, g_ref, sh_ref, sc_ref):
    g, up, sh = g_ref[...], 1.0 + sc_ref[...], sh_ref[...]

    def rows_fn(rows):
        x = h_ref[rows, :]
        ms = jnp.mean(x * x, axis=-1, keepdims=True)
        dst_ref[rows, :] = ((x * lax.rsqrt(ms + EPS) * g) * up + sh).astype(dst_ref.dtype)

    _row_chunks(h_ref.shape[0], rows_fn)


def _normmod_kernel(h_ref, g_ref, sh_ref, sc_ref, o_ref):
    _modulate_into(o_ref, h_ref, g_ref, sh_ref, sc_ref)


def _mod_spec(which, row_of):
    return pl.BlockSpec((None, None, 1, D_MODEL), lambda m, *_: (row_of(m), which, 0, 0))


def norm_modulate(h, g, mod, which_shift, which_scale, row_of, tm):
    m = h.shape[0]
    return pl.pallas_call(
        _normmod_kernel,
        grid=(m // tm,),
        in_specs=[pl.BlockSpec((tm, D_MODEL), lambda i: (i, 0)),
                  pl.BlockSpec((1, D_MODEL), lambda i: (0, 0)),
                  _mod_spec(which_shift, row_of), _mod_spec(which_scale, row_of)],
        out_specs=pl.BlockSpec((tm, D_MODEL), lambda i: (i, 0)),
        out_shape=jax.ShapeDtypeStruct((m, D_MODEL), BF16),
        compiler_params=_params(("arbitrary",)),
        name="norm_modulate",
    )(h, g.reshape(1, D_MODEL), mod, mod)


def _proj_kernel(x_ref, w_ref, o_ref):
    acc = jnp.dot(x_ref[...], w_ref[...], preferred_element_type=F32)
    for j in range(o_ref.shape[0]):
        o_ref[j] = acc[:, j * HEAD_DIM:(j + 1) * HEAD_DIM].astype(o_ref.dtype)


def project_heads(x, w, tm, tn):
    m, k = x.shape
    n = w.shape[1]
    return pl.pallas_call(
        _proj_kernel,
        grid=(n // tn, m // tm),
        in_specs=[pl.BlockSpec((tm, k), lambda j, i: (i, 0)),
                  pl.BlockSpec((k, tn), lambda j, i: (0, j))],
        out_specs=pl.BlockSpec((tn // HEAD_DIM, tm, HEAD_DIM), lambda j, i: (j, i, 0)),
        out_shape=jax.ShapeDtypeStruct((n // HEAD_DIM, m, HEAD_DIM), BF16),
        compiler_params=_params(("arbitrary", "arbitrary")),
        name="project_heads",
    )(x, w)


def _outproj_kernel(*refs, n_x):
    x_refs, (w_ref, h_ref, g_ref, o_ref) = refs[:n_x], refs[n_x:]
    acc = None
    off = 0
    for x_ref in x_refs:
        kk = x_ref.shape[1]
        part = jnp.dot(x_ref[...], w_ref[off:off + kk, :], preferred_element_type=F32)
        acc = part if acc is None else acc + part
        off += kk
    o_ref[...] = h_ref[...] + g_ref[...] * acc


def out_project_residual(xs, w, h, mod, which_gate, row_of, tm, tn):
    m = h.shape[0]
    k = w.shape[0]
    in_specs = [pl.BlockSpec((tm, x.shape[1]), lambda j, i: (i, 0)) for x in xs]
    in_specs += [pl.BlockSpec((k, tn), lambda j, i: (0, j)),
                 pl.BlockSpec((tm, tn), lambda j, i: (i, j)),
                 pl.BlockSpec((None, None, 1, tn), lambda j, i: (row_of(i), which_gate, 0, j))]
    return pl.pallas_call(
        functools.partial(_outproj_kernel, n_x=len(xs)),
        grid=(D_MODEL // tn, m // tm),
        in_specs=in_specs,
        out_specs=pl.BlockSpec((tm, tn), lambda j, i: (i, j)),
        out_shape=jax.ShapeDtypeStruct((m, D_MODEL), F32),
        compiler_params=_params(("arbitrary", "arbitrary")),
        name="out_project_residual",
    )(*xs, w, h, mod)


def _mlp_kernel(h_ref, gn_ref, sh_ref, sc_ref, gate_ref, w1_ref, w2_ref, fg_ref, o_ref, xs_ref, *, final_norm):
    j = pl.program_id(1)

    @pl.when(j == 0)
    def _():
        _modulate_into(xs_ref, h_ref, gn_ref, sh_ref, sc_ref)
        o_ref[...] = jnp.zeros_like(o_ref)

    hid = jnp.dot(xs_ref[...], w1_ref[...], preferred_element_type=F32)
    hid = jnp.square(jnp.maximum(hid, 0.0)).astype(BF16)
    o_ref[...] += jnp.dot(hid, w2_ref[...], preferred_element_type=F32)

    @pl.when(j == pl.num_programs(1) - 1)
    def _():
        gate, fg = gate_ref[...], fg_ref[...]

        def rows_fn(rows):
            y = h_ref[rows, :] + gate * o_ref[rows, :]
            if final_norm:
                ms = jnp.mean(y * y, axis=-1, keepdims=True)
                y = y * lax.rsqrt(ms + EPS) * fg
            o_ref[rows, :] = y

        _row_chunks(h_ref.shape[0], rows_fn)


def mlp_residual(h, gn, mod, row_of, w1, w2, final_g, final_norm, tm, th):
    m = h.shape[0]
    once = pl.Buffered(1)
    row = lambda i, j: (i, 0)
    return pl.pallas_call(
        functools.partial(_mlp_kernel, final_norm=final_norm),
        grid=(m // tm, MLP_HIDDEN // th),
        in_specs=[pl.BlockSpec((tm, D_MODEL), row, pipeline_mode=once),
                  pl.BlockSpec((1, D_MODEL), lambda i, j: (0, 0)),
                  _mod_spec(3, row_of), _mod_spec(4, row_of), _mod_spec(5, row_of),
                  pl.BlockSpec((D_MODEL, th), lambda i, j: (0, j)),
                  pl.BlockSpec((th, D_MODEL), lambda i, j: (j, 0)),
                  pl.BlockSpec((1, D_MODEL), lambda i, j: (0, 0))],
        out_specs=pl.BlockSpec((tm, D_MODEL), row, pipeline_mode=once),
        out_shape=jax.ShapeDtypeStruct((m, D_MODEL), F32),
        scratch_shapes=[pltpu.VMEM((tm, D_MODEL), BF16)],
        compiler_params=_params(("arbitrary", "arbitrary")),
        name="mlp_residual",
    )(h, gn.reshape(1, D_MODEL), mod, mod, mod, w1, w2, final_g.reshape(1, D_MODEL))


def _prep_kernel(x_ref, c_ref, s_ref, g_ref, o_ref):
    j = pl.program_id(1)
    x = x_ref[...].astype(F32)
    ms = jnp.mean(x * x, axis=-1, keepdims=True)
    inv = jnp.where(j < N_NORMED, lax.rsqrt(ms + EPS), 1.0)
    y = x * inv * g_ref[...]
    lane = lax.broadcasted_iota(jnp.int32, y.shape, 1)
    partner = jnp.where(lane % 64 < 32, pltpu.roll(y, LANES - 32, 1), pltpu.roll(y, 32, 1))
    o_ref[...] = (y * c_ref[...] + partner * s_ref[...]).astype(o_ref.dtype)


def prep_qk(proj, cos_t, sin_t, gains, tm):
    m = proj.shape[1]
    nt = cos_t.shape[0] // tm
    src = lambda j: jnp.where(j < N_NORMED, j, j + (P_QB - N_NORMED))
    return pl.pallas_call(
        _prep_kernel,
        grid=(m // tm, N_ROT),
        in_specs=[pl.BlockSpec((None, tm, HEAD_DIM), lambda i, j: (src(j), i, 0)),
                  pl.BlockSpec((tm, HEAD_DIM), lambda i, j: (i % nt, 0)),
                  pl.BlockSpec((tm, HEAD_DIM), lambda i, j: (i % nt, 0)),
                  pl.BlockSpec((None, 1, HEAD_DIM), lambda i, j: (j, 0, 0))],
        out_specs=pl.BlockSpec((None, tm, HEAD_DIM), lambda i, j: (j, i, 0)),
        out_shape=jax.ShapeDtypeStruct((N_ROT, m, HEAD_DIM), BF16),
        compiler_params=_params(("arbitrary", "arbitrary")),
        name="prep_qk",
    )(proj, cos_t, sin_t, gains)


def _nt_dot(a, b):
    return lax.dot_general(a, b, (((1,), (1,)), ((), ())), preferred_element_type=F32)


def _softmax_step(q, k, v, m_ref, l_ref, acc_ref):
    s = _nt_dot(q, k)
    m_prev = m_ref[...]
    m_new = jnp.maximum(m_prev, jnp.max(s, axis=1, keepdims=True))
    alpha = jnp.exp(m_prev - m_new)
    p = jnp.exp(s - m_new)
    l_ref[...] = alpha * l_ref[...] + jnp.sum(p, axis=1, keepdims=True)
    acc_ref[...] = alpha * acc_ref[...] + jnp.dot(p.astype(BF16), v, preferred_element_type=F32)
    m_ref[...] = m_new


def _gqa_kernel(*refs, with_lat, tk):
    if with_lat:
        q_ref, kl_ref, vl_ref, kc_ref, vc_ref, o_ref, m_ref, l_ref, acc_ref = refs
    else:
        q_ref, kc_ref, vc_ref, o_ref, m_ref, l_ref, acc_ref = refs
    g, tq, d = q_ref.shape
    q = q_ref[...].reshape(g * tq, d)
    m_ref[...] = jnp.full_like(m_ref, NEG)
    l_ref[...] = jnp.zeros_like(l_ref)
    acc_ref[...] = jnp.zeros_like(acc_ref)
    if with_lat:
        def body(c, carry):
            rows = pl.ds(pl.multiple_of(c * tk, tk), tk)
            _softmax_step(q, kl_ref[rows, :], vl_ref[rows, :], m_ref, l_ref, acc_ref)
            return carry
        lax.fori_loop(0, kl_ref.shape[0] // tk, body, 0)
    _softmax_step(q, kc_ref[...], vc_ref[...], m_ref, l_ref, acc_ref)
    o = acc_ref[...] * (1.0 / l_ref[...])
    for h in range(g):
        o_ref[:, h * d:(h + 1) * d] = o[h * tq:(h + 1) * tq].astype(o_ref.dtype)


def gqa_attention(rot_q, proj_q, rot_lat, proj_lat, rot_ctx, proj_ctx, tq, tk):
    mq = rot_q.shape[1]
    nq = mq // BATCH // tq
    with_lat = rot_lat is not None
    q_spec = pl.BlockSpec((A_GROUP, tq, HEAD_DIM), lambda b, g, i: (g, b * nq + i, 0))
    kv = lambda base, n: pl.BlockSpec((None, n, HEAD_DIM), lambda b, g, i: (base + g, b, 0))
    in_specs, args = [q_spec], [rot_q]
    if with_lat:
        in_specs += [kv(R_KA, SEQ), kv(P_VA, SEQ)]
        args += [rot_lat, proj_lat]
    in_specs += [kv(R_KA, CTX_LEN), kv(P_VA, CTX_LEN)]
    args += [rot_ctx, proj_ctx]
    rows = A_GROUP * tq
    return pl.pallas_call(
        functools.partial(_gqa_kernel, with_lat=with_lat, tk=tk),
        grid=(BATCH, A_KV_HEADS, nq),
        in_specs=in_specs,
        out_specs=pl.BlockSpec((tq, A_GROUP * HEAD_DIM), lambda b, g, i: (b * nq + i, g)),
        out_shape=jax.ShapeDtypeStruct((mq, A_HEADS * HEAD_DIM), BF16),
        scratch_shapes=[pltpu.VMEM((rows, 1), F32), pltpu.VMEM((rows, 1), F32),
                        pltpu.VMEM((rows, HEAD_DIM), F32)],
        compiler_params=_params(("arbitrary", "arbitrary", "arbitrary")),
        name="gqa_attention",
    )(*args)


def _diff_kernel(*refs, with_lat, tk, lambda_init):
    if with_lat:
        (q_ref, kl_ref, vl_ref, kc_ref, vc_ref, lq1_ref, lk1_ref, lq2_ref, lk2_ref, sg_ref,
         o_ref, m_ref, l_ref, acc_ref) = refs
    else:
        (q_ref, kc_ref, vc_ref, lq1_ref, lk1_ref, lq2_ref, lk2_ref, sg_ref,
         o_ref, m_ref, l_ref, acc_ref) = refs
    m_ref[...] = jnp.full_like(m_ref, NEG)
    l_ref[...] = jnp.zeros_like(l_ref)
    acc_ref[...] = jnp.zeros_like(acc_ref)

    def step(k_of, v):
        for i in range(2):
            _softmax_step(q_ref[i], k_of(i), v, m_ref.at[i], l_ref.at[i], acc_ref.at[i])

    if with_lat:
        def body(c, carry):
            rows = pl.ds(pl.multiple_of(c * tk, tk), tk)
            v = jnp.concatenate([vl_ref[0, rows, :], vl_ref[1, rows, :]], axis=1)
            step(lambda i: kl_ref[i, rows, :], v)
            return carry
        lax.fori_loop(0, kl_ref.shape[1] // tk, body, 0)
    step(lambda i: kc_ref[i], jnp.concatenate([vc_ref[0], vc_ref[1]], axis=1))

    lam = (jnp.exp(jnp.sum(lq1_ref[...] * lk1_ref[...], axis=1, keepdims=True))
           - jnp.exp(jnp.sum(lq2_ref[...] * lk2_ref[...], axis=1, keepdims=True)) + lambda_init)
    o = acc_ref[0] * (1.0 / l_ref[0]) - lam * (acc_ref[1] * (1.0 / l_ref[1]))
    ms = jnp.mean(o * o, axis=-1, keepdims=True)
    o_ref[...] = ((o * lax.rsqrt(ms + EPS) * sg_ref[...]) * (1.0 - lambda_init)).astype(o_ref.dtype)


def diff_attention(rot_q, rot_lat, proj_lat, rot_ctx, proj_ctx, lam_params, subln_g, lambda_init, tq, tk):
    mq = rot_q.shape[1]
    nq = mq // BATCH // tq
    with_lat = rot_lat is not None
    pair = lambda base, n: pl.BlockSpec((2, n, HEAD_DIM), lambda b, h, i: (base // 2 + h, b, 0))
    in_specs = [pl.BlockSpec((2, tq, HEAD_DIM), lambda b, h, i: (R_QB // 2 + h, b * nq + i, 0))]
    args = [rot_q]
    if with_lat:
        in_specs += [pair(R_KB, SEQ), pair(P_VB, SEQ)]
        args += [rot_lat, proj_lat]
    in_specs += [pair(R_KB, CTX_LEN), pair(P_VB, CTX_LEN)]
    args += [rot_ctx, proj_ctx]
    vec = lambda n: pl.BlockSpec((1, n), lambda b, h, i: (0, 0))
    in_specs += [vec(HEAD_DIM)] * 4 + [vec(2 * HEAD_DIM)]
    args += [p.reshape(1, HEAD_DIM) for p in lam_params] + [subln_g.reshape(1, 2 * HEAD_DIM)]
    return pl.pallas_call(
        functools.partial(_diff_kernel, with_lat=with_lat, tk=tk, lambda_init=lambda_init),
        grid=(BATCH, B_HEADS, nq),
        in_specs=in_specs,
        out_specs=pl.BlockSpec((tq, 2 * HEAD_DIM), lambda b, h, i: (b * nq + i, h)),
        out_shape=jax.ShapeDtypeStruct((mq, B_HEADS * 2 * HEAD_DIM), BF16),
        scratch_shapes=[pltpu.VMEM((2, tq, 1), F32), pltpu.VMEM((2, tq, 1), F32),
                        pltpu.VMEM((2, tq, 2 * HEAD_DIM), F32)],
        compiler_params=_params(("arbitrary", "arbitrary", "arbitrary")),
        name="diff_attention",
    )(*args)


NA_BAND = NA_ROWS * GRID_W


def _na_kernel(q_ref, k_ref, v_ref, kc_ref, vc_ref, bias_ref, o_ref):
    kc = kc_ref[...]
    vc = vc_ref[...]

    def row(r, carry):
        r0 = jnp.clip(r - NA_ROWS // 2, 0, ROWS - NA_ROWS)
        q = q_ref[pl.ds(pl.multiple_of(r * GRID_W, GRID_W), GRID_W), :]
        band = pl.ds(pl.multiple_of(r0 * GRID_W, GRID_W), NA_BAND)
        base = r0 - r + (NA_ROWS - 1)
        bias = jnp.concatenate([bias_ref[base + 2 * p] for p in range(NA_ROWS // 2)], axis=1)
        s_loc = _nt_dot(q, k_ref[band, :]) * SCALE + bias
        s_ctx = _nt_dot(q, kc) * SCALE
        m = jnp.maximum(jnp.max(s_loc, axis=1, keepdims=True), jnp.max(s_ctx, axis=1, keepdims=True))
        p_loc = jnp.exp(s_loc - m)
        p_ctx = jnp.exp(s_ctx - m)
        l = jnp.sum(p_loc, axis=1, keepdims=True) + jnp.sum(p_ctx, axis=1, keepdims=True)
        o = (jnp.dot(p_loc.astype(BF16), v_ref[band, :], preferred_element_type=F32)
             + jnp.dot(p_ctx.astype(BF16), vc, preferred_element_type=F32))
        o_ref[pl.ds(pl.multiple_of(r * GRID_W, GRID_W), GRID_W), :] = (o * (1.0 / l)).astype(o_ref.dtype)
        return carry

    lax.fori_loop(0, ROWS, row, 0)


def neighbourhood_attention(proj_lat, proj_ctx, bias_pairs):
    lat = lambda base: pl.BlockSpec((None, SEQ, HEAD_DIM), lambda b, h: (base + h, b, 0))
    ctx = lambda base: pl.BlockSpec((None, CTX_LEN, HEAD_DIM), lambda b, h: (base + h, b, 0))
    return pl.pallas_call(
        _na_kernel,
        grid=(BATCH, C_HEADS),
        in_specs=[lat(0), lat(C_HEADS), lat(2 * C_HEADS), ctx(0), ctx(C_HEADS),
                  pl.BlockSpec((None, 2 * NA_ROWS - 2, GRID_W, 2 * GRID_W), lambda b, h: (h, 0, 0, 0))],
        out_specs=pl.BlockSpec((SEQ, HEAD_DIM), lambda b, h: (b, h)),
        out_shape=jax.ShapeDtypeStruct((BATCH * SEQ, C_HEADS * HEAD_DIM), BF16),
        compiler_params=_params(("arbitrary", "arbitrary")),
        name="neighbourhood_attention",
    )(proj_lat, proj_lat, proj_lat, proj_ctx, proj_ctx, bias_pairs)


def _rope_tables():
    t = jnp.arange(SEQ, dtype=jnp.int32)
    row = (t // GRID_W).astype(F32)
    col = (t % GRID_W).astype(F32)
    n_freq = HEAD_DIM // 4
    freqs = ROPE_THETA ** (-jnp.arange(n_freq, dtype=F32) / n_freq)
    ar, ac = row[:, None] * freqs, col[:, None] * freqs
    cos_t = jnp.concatenate([jnp.cos(ar), jnp.cos(ar), jnp.cos(ac), jnp.cos(ac)], axis=1)
    sin_t = jnp.concatenate([-jnp.sin(ar), jnp.sin(ar), -jnp.sin(ac), jnp.sin(ac)], axis=1)
    return cos_t, sin_t


def _na_bias_pairs(rel_bias):
    col = jnp.arange(GRID_W, dtype=jnp.int32)
    c0 = jnp.clip(col - NA_COLS // 2, 0, GRID_W - NA_COLS)
    dc = col[None, :] - col[:, None] + (NA_COLS - 1)
    valid = (col[None, :] >= c0[:, None]) & (col[None, :] < c0[:, None] + NA_COLS)
    t = rel_bias[:, :, jnp.clip(dc, 0, 2 * NA_COLS - 2)]
    t = jnp.where(valid[None, None], t, NEG).astype(F32)
    return jnp.concatenate([t[:, :-1], t[:, 1:]], axis=-1)


def kernel(x, c, ctx, c_ctx, ada_w, ada_b, norm1_g, norm2_g, w_in_even, w_out_even, a_q_norm, a_k_norm,
           b_lambda_q1, b_lambda_k1, b_lambda_q2, b_lambda_k2, b_subln_g, w_in_odd, w_out_odd, na_rel_bias,
           mlp_w1, mlp_w2, final_g):
    TM = 1024
    TC = BATCH * CTX_LEN
    TN = 512
    lat_row = lambda m: m // (SEQ // TM)
    lat_row_n = lambda m: m // (SEQ // TN)
    ctx_row = lambda m: 2

    craw = jnp.concatenate([c, c_ctx[None], jnp.zeros((5, D_MODEL), F32)], axis=0)
    mod = ada_modulation(craw, ada_w, ada_b).reshape(DEPTH, 8, 6, 1, D_MODEL)

    h = x.reshape(BATCH * SEQ, D_MODEL)
    hc = ctx.reshape(TC, D_MODEL)
    cos_t, sin_t = _rope_tables()
    one_t, zero_t = jnp.ones((TC, HEAD_DIM), F32), jnp.zeros((TC, HEAD_DIM), F32)

    mod0 = mod[0]
    w_in = w_in_even[0].astype(BF16)
    w_out = w_out_even[0].astype(BF16)
    ones = jnp.ones((HEAD_DIM,), F32)
    gains = jnp.stack([a_q_norm[0] * SCALE] * A_HEADS + [a_k_norm[0]] * A_KV_HEADS
                      + [ones * SCALE] * (2 * B_HEADS) + [ones] * (2 * B_HEADS)).reshape(N_ROT, 1, HEAD_DIM)
    lam_params = (b_lambda_q1[0], b_lambda_k1[0], b_lambda_q2[0], b_lambda_k2[0])
    lambda_init = 0.8 - 0.6 * math.exp(-0.3 * 0)

    u = norm_modulate(h, norm1_g[0], mod0, 0, 1, lat_row_n, TN)
    uc = norm_modulate(hc, norm1_g[0], mod0, 0, 1, ctx_row, TC)
    proj = project_heads(u, w_in, TM, 1024)
    proj_c = project_heads(uc, w_in, TC, 1024)
    rot = prep_qk(proj, cos_t, sin_t, gains, 2048)
    rot_c = prep_qk(proj_c, one_t, zero_t, gains, TC)

    a_lat = gqa_attention(rot, proj, rot, proj, rot_c, proj_c, 128, 512)
    b_lat = diff_attention(rot, rot, proj, rot_c, proj_c, lam_params, b_subln_g[0], lambda_init, 512, 512)
    a_ctx = gqa_attention(rot_c, proj_c, None, None, rot_c, proj_c, 128, 512)
    b_ctx = diff_attention(rot_c, None, None, rot_c, proj_c, lam_params, b_subln_g[0], lambda_init, 256, 512)

    h = out_project_residual([a_lat, b_lat], w_out, h, mod0, 2, lat_row, TM, 1024)
    hc = out_project_residual([a_ctx, b_ctx], w_out, hc, mod0, 2, ctx_row, TC, 1024)
    w1 = mlp_w1[0].astype(BF16)
    w2 = mlp_w2[0].astype(BF16)
    h = mlp_residual(h, norm2_g[0], mod0, lat_row, w1, w2, final_g, False, TM, 256)
    hc = mlp_residual(hc, norm2_g[0], mod0, ctx_row, w1, w2, final_g, False, TC, 256)

    mod1 = mod[1]
    w_in = w_in_odd[0].astype(BF16)
    u = norm_modulate(h, norm1_g[1], mod1, 0, 1, lat_row_n, TN)
    uc = norm_modulate(hc, norm1_g[1], mod1, 0, 1, ctx_row, TC)
    proj = project_heads(u, w_in, TM, 1024)
    proj_c = project_heads(uc, w_in[:, C_HEADS * HEAD_DIM:], TC, 1024)
    attn = neighbourhood_attention(proj, proj_c, _na_bias_pairs(na_rel_bias[0]))
    h = out_project_residual([attn], w_out_odd[0].astype(BF16), h, mod1, 2, lat_row, TM, 1024)
    out = mlp_residual(h, norm2_g[1], mod1, lat_row, mlp_w1[1].astype(BF16), mlp_w2[1].astype(BF16),
                       final_g, True, TM, 256)
    return out.reshape(BATCH, SEQ, D_MODEL)
```

```python
import functools
import math

import jax
import jax.numpy as jnp
from jax import lax
from jax.experimental import pallas as pl
from jax.experimental.pallas import tpu as pltpu

D_MODEL = 4096
BATCH = 2
SEQ = 8192
DEPTH = 2
GRID_W = 64
ROWS = SEQ // GRID_W
CTX_LEN = 256
HEAD_DIM = 128
ROPE_THETA = 10000.0
EPS = 1e-6
A_HEADS = 16
A_KV_HEADS = 4
A_GROUP = A_HEADS // A_KV_HEADS
B_HEADS = 8
C_HEADS = 32
NA_ROWS = 8
NA_COLS = 16
MLP_HIDDEN = 4 * D_MODEL
EVEN_IN = 9216
SCALE = HEAD_DIM ** -0.5
LOG2E = 1.4426950408889634
SCALE_LOG2 = SCALE * LOG2E

P_QA, P_KA, P_VA, P_QB, P_KB, P_VB = 0, 16, 20, 24, 40, 56
N_PROJ_EVEN = EVEN_IN // HEAD_DIM
R_QA, R_KA, R_QB, R_KB = 0, 16, 20, 36
N_ROT = 52
N_NORMED = 20

LANES = 128
VMEM_LIMIT = 56 * 1024 * 1024
NEG = -1e30

F32 = jnp.float32
BF16 = jnp.bfloat16


def _params(sem, vmem=VMEM_LIMIT):
    return pltpu.CompilerParams(dimension_semantics=sem, vmem_limit_bytes=vmem)


def _ada_kernel(c_ref, w_ref, b_ref, o_ref):
    x = c_ref[...]
    cond = x / (1.0 + jnp.exp(-x))
    o_ref[...] = jnp.dot(cond.astype(BF16), w_ref[...].astype(BF16),
                         preferred_element_type=F32) + b_ref[...]


def ada_modulation(craw, ada_w, ada_b):
    tn = 512
    n6 = 6 * D_MODEL
    return pl.pallas_call(
        _ada_kernel,
        grid=(DEPTH, n6 // tn),
        in_specs=[pl.BlockSpec((8, D_MODEL), lambda i, n: (0, 0)),
                  pl.BlockSpec((None, D_MODEL, tn), lambda i, n: (i, 0, n)),
                  pl.BlockSpec((None, 1, tn), lambda i, n: (i, 0, n))],
        out_specs=pl.BlockSpec((None, 8, tn), lambda i, n: (i, 0, n)),
        out_shape=jax.ShapeDtypeStruct((DEPTH, 8, n6), F32),
        compiler_params=_params(("arbitrary", "arbitrary")),
        name="ada_modulation",
    )(craw, ada_w, ada_b.reshape(DEPTH, 1, n6))


ROW_CHUNK = 16


def _row_chunks(n_rows, fn):
    def body(i, carry):
        fn(pl.ds(pl.multiple_of(i * ROW_CHUNK, ROW_CHUNK), ROW_CHUNK))
        return carry
    lax.fori_loop(0, n_rows // ROW_CHUNK, body, 0)


def _modulate_into(dst_ref, h_ref, g_ref, sh_ref, sc_ref):
    g, up, sh = g_ref[...], 1.0 + sc_ref[...], sh_ref[...]

    def rows_fn(rows):
        x = h_ref[rows, :]
        ms = jnp.mean(x * x, axis=-1, keepdims=True)
        dst_ref[rows, :] = ((x * lax.rsqrt(ms + EPS) * g) * up + sh).astype(dst_ref.dtype)

    _row_chunks(h_ref.shape[0], rows_fn)


def _normmod_kernel(h_ref, g_ref, sh_ref, sc_ref, o_ref):
    _modulate_into(o_ref, h_ref, g_ref, sh_ref, sc_ref)


def _mod_spec(which, row_of):
    return pl.BlockSpec((None, None, 1, D_MODEL), lambda m, *_: (row_of(m), which, 0, 0))


def norm_modulate(h, g, mod, which_shift, which_scale, row_of, tm):
    m = h.shape[0]
    return pl.pallas_call(
        _normmod_kernel,
        grid=(m // tm,),
        in_specs=[pl.BlockSpec((tm, D_MODEL), lambda i: (i, 0)),
                  pl.BlockSpec((1, D_MODEL), lambda i: (0, 0)),
                  _mod_spec(which_shift, row_of), _mod_spec(which_scale, row_of)],
        out_specs=pl.BlockSpec((tm, D_MODEL), lambda i: (i, 0)),
        out_shape=jax.ShapeDtypeStruct((m, D_MODEL), BF16),
        compiler_params=_params(("arbitrary",)),
        name="norm_modulate",
    )(h, g.reshape(1, D_MODEL), mod, mod)


def _proj_kernel(x_ref, w_ref, o_ref):
    acc = jnp.dot(x_ref[...], w_ref[...], preferred_element_type=F32)
    for j in range(o_ref.shape[0]):
        o_ref[j] = acc[:, j * HEAD_DIM:(j + 1) * HEAD_DIM].astype(o_ref.dtype)


def project_heads(x, w, tm, tn):
    m, k = x.shape
    n = w.shape[1]
    return pl.pallas_call(
        _proj_kernel,
        grid=(n // tn, m // tm),
        in_specs=[pl.BlockSpec((tm, k), lambda j, i: (i, 0)),
                  pl.BlockSpec((k, tn), lambda j, i: (0, j))],
        out_specs=pl.BlockSpec((tn // HEAD_DIM, tm, HEAD_DIM), lambda j, i: (j, i, 0)),
        out_shape=jax.ShapeDtypeStruct((n // HEAD_DIM, m, HEAD_DIM), BF16),
        compiler_params=_params(("arbitrary", "arbitrary")),
        name="project_heads",
    )(x, w)


def _outproj_kernel(*refs, n_x):
    x_refs, (w_ref, h_ref, g_ref, o_ref) = refs[:n_x], refs[n_x:]
    acc = None
    off = 0
    for x_ref in x_refs:
        kk = x_ref.shape[1]
        part = jnp.dot(x_ref[...], w_ref[off:off + kk, :], preferred_element_type=F32)
        acc = part if acc is None else acc + part
        off += kk
    o_ref[...] = h_ref[...] + g_ref[...] * acc


def out_project_residual(xs, w, h, mod, which_gate, row_of, tm, tn):
    m = h.shape[0]
    k = w.shape[0]
    in_specs = [pl.BlockSpec((tm, x.shape[1]), lambda j, i: (i, 0)) for x in xs]
    in_specs += [pl.BlockSpec((k, tn), lambda j, i: (0, j)),
                 pl.BlockSpec((tm, tn), lambda j, i: (i, j)),
                 pl.BlockSpec((None, None, 1, tn), lambda j, i: (row_of(i), which_gate, 0, j))]
    return pl.pallas_call(
        functools.partial(_outproj_kernel, n_x=len(xs)),
        grid=(D_MODEL // tn, m // tm),
        in_specs=in_specs,
        out_specs=pl.BlockSpec((tm, tn), lambda j, i: (i, j)),
        out_shape=jax.ShapeDtypeStruct((m, D_MODEL), F32),
        compiler_params=_params(("arbitrary", "arbitrary")),
        name="out_project_residual",
    )(*xs, w, h, mod)


def _mlp_kernel(h_ref, gn_ref, sh_ref, sc_ref, gate_ref, w1_ref, w2_ref, fg_ref, o_ref, xs_ref, hid_ref, *,
                final_norm):
    j = pl.program_id(1)

    @pl.when(j == 0)
    def _():
        _modulate_into(xs_ref, h_ref, gn_ref, sh_ref, sc_ref)
        o_ref[...] = jnp.zeros_like(o_ref)
        hid_ref[...] = jnp.zeros_like(hid_ref)

    o_ref[...] = jnp.dot(hid_ref[...], w2_ref[...], preferred_element_type=F32) + o_ref[...]
    hid = jnp.dot(xs_ref[...], w1_ref[...], preferred_element_type=F32)
    hid_ref[...] = jnp.square(jnp.maximum(hid, 0.0)).astype(BF16)

    @pl.when(j == pl.num_programs(1) - 1)
    def _():
        gate, fg = gate_ref[...], fg_ref[...]

        def rows_fn(rows):
            y = h_ref[rows, :] + gate * o_ref[rows, :]
            if final_norm:
                ms = jnp.mean(y * y, axis=-1, keepdims=True)
                y = y * lax.rsqrt(ms + EPS) * fg
            o_ref[rows, :] = y

        _row_chunks(h_ref.shape[0], rows_fn)


def mlp_residual(h, gn, mod, row_of, w1, w2, final_g, final_norm, tm, th):
    m = h.shape[0]
    nj = MLP_HIDDEN // th
    once = pl.Buffered(1)
    row = lambda i, j: (i, 0)
    return pl.pallas_call(
        functools.partial(_mlp_kernel, final_norm=final_norm),
        grid=(m // tm, nj + 1),
        in_specs=[pl.BlockSpec((tm, D_MODEL), row, pipeline_mode=once),
                  pl.BlockSpec((1, D_MODEL), lambda i, j: (0, 0)),
                  _mod_spec(3, row_of), _mod_spec(4, row_of), _mod_spec(5, row_of),
                  pl.BlockSpec((D_MODEL, th), lambda i, j: (0, jnp.minimum(j, nj - 1))),
                  pl.BlockSpec((th, D_MODEL), lambda i, j: (jnp.maximum(j - 1, 0), 0)),
                  pl.BlockSpec((1, D_MODEL), lambda i, j: (0, 0))],
        out_specs=pl.BlockSpec((tm, D_MODEL), row, pipeline_mode=once),
        out_shape=jax.ShapeDtypeStruct((m, D_MODEL), F32),
        scratch_shapes=[pltpu.VMEM((tm, D_MODEL), BF16), pltpu.VMEM((tm, th), BF16)],
        compiler_params=_params(("arbitrary", "arbitrary")),
        name="mlp_residual",
    )(h, gn.reshape(1, D_MODEL), mod, mod, mod, w1, w2, final_g.reshape(1, D_MODEL))


def _prep_kernel(x_ref, c_ref, s_ref, g_ref, o_ref):
    j = pl.program_id(1)
    x = x_ref[...].astype(F32)
    ms = jnp.mean(x * x, axis=-1, keepdims=True)
    inv = jnp.where(j < N_NORMED, lax.rsqrt(ms + EPS), 1.0)
    y = x * inv * g_ref[...]
    lane = lax.broadcasted_iota(jnp.int32, y.shape, 1)
    partner = jnp.where(lane % 64 < 32, pltpu.roll(y, LANES - 32, 1), pltpu.roll(y, 32, 1))
    o_ref[...] = (y * c_ref[...] + partner * s_ref[...]).astype(o_ref.dtype)


def prep_qk(proj, cos_t, sin_t, gains, tm):
    m = proj.shape[1]
    nt = cos_t.shape[0] // tm
    src = lambda j: jnp.where(j < N_NORMED, j, j + (P_QB - N_NORMED))
    return pl.pallas_call(
        _prep_kernel,
        grid=(m // tm, N_ROT),
        in_specs=[pl.BlockSpec((None, tm, HEAD_DIM), lambda i, j: (src(j), i, 0)),
                  pl.BlockSpec((tm, HEAD_DIM), lambda i, j: (i % nt, 0)),
                  pl.BlockSpec((tm, HEAD_DIM), lambda i, j: (i % nt, 0)),
                  pl.BlockSpec((None, 1, HEAD_DIM), lambda i, j: (j, 0, 0))],
        out_specs=pl.BlockSpec((None, tm, HEAD_DIM), lambda i, j: (j, i, 0)),
        out_shape=jax.ShapeDtypeStruct((N_ROT, m, HEAD_DIM), BF16),
        compiler_params=_params(("arbitrary", "arbitrary")),
        name="prep_qk",
    )(proj, cos_t, sin_t, gains)


def _nt_dot(a, b):
    return lax.dot_general(a, b, (((1,), (1,)), ((), ())), preferred_element_type=F32)


def _transpose_to(dst_ref, src):
    dst_ref[...] = src.astype(F32).T.astype(dst_ref.dtype)


def _scores_to_probs(q, k, m_ref, l_ref, p_ref, alpha_ref):
    s = _nt_dot(k, q)
    m_prev = m_ref[...]
    m_new = jnp.maximum(m_prev, jnp.max(s, axis=0, keepdims=True))
    alpha = jnp.exp2(m_prev - m_new)
    p = jnp.exp2(s - m_new)
    l_ref[...] = alpha * l_ref[...] + jnp.sum(p, axis=0, keepdims=True)
    m_ref[...] = m_new
    p_ref[...] = p.astype(BF16)
    alpha_ref[...] = alpha


def _probs_to_acc(vt, p_ref, alpha_ref, acc_ref):
    acc_ref[...] = alpha_ref[...] * acc_ref[...] + jnp.dot(vt, p_ref[...], preferred_element_type=F32)


def _gqa_kernel(*refs, with_lat, tk):
    if with_lat:
        (q_ref, kl_ref, vl_ref, kc_ref, vc_ref, o_ref,
         vlt_ref, vct_ref, m_ref, l_ref, acc_ref, p_ref, alpha_ref) = refs
    else:
        q_ref, kc_ref, vc_ref, o_ref, vct_ref, m_ref, l_ref, acc_ref, p_ref, alpha_ref = refs
    g, tq, d = q_ref.shape
    n_ctx = kc_ref.shape[0]

    @pl.when(pl.program_id(2) == 0)
    def _():
        _transpose_to(vct_ref, vc_ref[...])
        if with_lat:
            def body(c, carry):
                _transpose_to(vlt_ref.at[c], vl_ref[pl.ds(pl.multiple_of(c * tk, tk), tk), :])
                return carry
            lax.fori_loop(0, vlt_ref.shape[0], body, 0)

    q = q_ref[...].reshape(g * tq, d)
    m_ref[...] = jnp.full_like(m_ref, NEG)
    l_ref[...] = jnp.zeros_like(l_ref)
    acc_ref[...] = jnp.zeros_like(acc_ref)
    if with_lat:
        n_chunks = vlt_ref.shape[0]
        rows = lambda c: pl.ds(pl.multiple_of(c * tk, tk), tk)
        _scores_to_probs(q, kl_ref[rows(0), :], m_ref, l_ref, p_ref, alpha_ref)

        def body(c, carry):
            _probs_to_acc(vlt_ref[c - 1], p_ref, alpha_ref, acc_ref)
            _scores_to_probs(q, kl_ref[rows(c), :], m_ref, l_ref, p_ref, alpha_ref)
            return carry
        lax.fori_loop(1, n_chunks, body, 0)
        _probs_to_acc(vlt_ref[n_chunks - 1], p_ref, alpha_ref, acc_ref)
    pc_ref = p_ref.at[0:n_ctx, :]
    _scores_to_probs(q, kc_ref[...], m_ref, l_ref, pc_ref, alpha_ref)
    _probs_to_acc(vct_ref[...], pc_ref, alpha_ref, acc_ref)
    o = (acc_ref[...] * (1.0 / l_ref[...])).T
    for h in range(g):
        o_ref[:, h * d:(h + 1) * d] = o[h * tq:(h + 1) * tq].astype(o_ref.dtype)


def gqa_attention(rot_q, proj_q, rot_lat, proj_lat, rot_ctx, proj_ctx, tq, tk):
    mq = rot_q.shape[1]
    nq = mq // BATCH // tq
    with_lat = rot_lat is not None
    q_spec = pl.BlockSpec((A_GROUP, tq, HEAD_DIM), lambda b, g, i: (g, b * nq + i, 0))
    kv = lambda base, n: pl.BlockSpec((None, n, HEAD_DIM), lambda b, g, i: (base + g, b, 0))
    in_specs, args = [q_spec], [rot_q]
    if with_lat:
        in_specs += [kv(R_KA, SEQ), kv(P_VA, SEQ)]
        args += [rot_lat, proj_lat]
    in_specs += [kv(R_KA, CTX_LEN), kv(P_VA, CTX_LEN)]
    args += [rot_ctx, proj_ctx]
    rows = A_GROUP * tq
    scratch = [pltpu.VMEM((SEQ // tk, HEAD_DIM, tk), BF16)] if with_lat else []
    n_p = max(tk, CTX_LEN) if with_lat else CTX_LEN
    scratch += [pltpu.VMEM((HEAD_DIM, CTX_LEN), BF16),
                pltpu.VMEM((1, rows), F32), pltpu.VMEM((1, rows), F32), pltpu.VMEM((HEAD_DIM, rows), F32),
                pltpu.VMEM((n_p, rows), BF16), pltpu.VMEM((1, rows), F32)]
    return pl.pallas_call(
        functools.partial(_gqa_kernel, with_lat=with_lat, tk=tk),
        grid=(BATCH, A_KV_HEADS, nq),
        in_specs=in_specs,
        out_specs=pl.BlockSpec((tq, A_GROUP * HEAD_DIM), lambda b, g, i: (b * nq + i, g)),
        out_shape=jax.ShapeDtypeStruct((mq, A_HEADS * HEAD_DIM), BF16),
        scratch_shapes=scratch,
        compiler_params=_params(("arbitrary", "arbitrary", "arbitrary")),
        name="gqa_attention",
    )(*args)


def _diff_kernel(*refs, with_lat, tk, lambda_init):
    if with_lat:
        (q_ref, kl_ref, vl_ref, kc_ref, vc_ref, lq1_ref, lk1_ref, lq2_ref, lk2_ref, sg_ref,
         o_ref, vlt_ref, vct_ref, m_ref, l_ref, acc_ref, p_ref, alpha_ref) = refs
    else:
        (q_ref, kc_ref, vc_ref, lq1_ref, lk1_ref, lq2_ref, lk2_ref, sg_ref,
         o_ref, vct_ref, m_ref, l_ref, acc_ref, p_ref, alpha_ref) = refs
    d = q_ref.shape[2]

    @pl.when(pl.program_id(2) == 0)
    def _():
        for half in range(2):
            _transpose_to(vct_ref.at[half * d:(half + 1) * d, :], vc_ref[half])
        if with_lat:
            def body(c, carry):
                rows = pl.ds(pl.multiple_of(c * tk, tk), tk)
                for half in range(2):
                    _transpose_to(vlt_ref.at[c, half * d:(half + 1) * d, :], vl_ref[half, rows, :])
                return carry
            lax.fori_loop(0, vlt_ref.shape[0], body, 0)

    m_ref[...] = jnp.full_like(m_ref, NEG)
    l_ref[...] = jnp.zeros_like(l_ref)
    acc_ref[...] = jnp.zeros_like(acc_ref)

    def to_probs(k_of, p_of):
        for i in range(2):
            _scores_to_probs(q_ref[i], k_of(i), m_ref.at[i], l_ref.at[i], p_of(i), alpha_ref.at[i])

    def to_acc(vt, p_of):
        for i in range(2):
            _probs_to_acc(vt, p_of(i), alpha_ref.at[i], acc_ref.at[i])

    if with_lat:
        n_chunks = vlt_ref.shape[0]
        rows = lambda c: pl.ds(pl.multiple_of(c * tk, tk), tk)
        p_lat = lambda i: p_ref.at[i]
        to_probs(lambda i: kl_ref[i, rows(0), :], p_lat)

        def body(c, carry):
            to_acc(vlt_ref[c - 1], p_lat)
            to_probs(lambda i: kl_ref[i, rows(c), :], p_lat)
            return carry
        lax.fori_loop(1, n_chunks, body, 0)
        to_acc(vlt_ref[n_chunks - 1], p_lat)
    p_ctx = lambda i: p_ref.at[i, 0:kc_ref.shape[1], :]
    to_probs(lambda i: kc_ref[i], p_ctx)
    to_acc(vct_ref[...], p_ctx)

    lam = (jnp.exp(jnp.sum(lq1_ref[...] * lk1_ref[...], axis=1, keepdims=True))
           - jnp.exp(jnp.sum(lq2_ref[...] * lk2_ref[...], axis=1, keepdims=True)) + lambda_init)
    o = (acc_ref[0] * (1.0 / l_ref[0]) - lam * (acc_ref[1] * (1.0 / l_ref[1]))).T
    ms = jnp.mean(o * o, axis=-1, keepdims=True)
    o_ref[...] = ((o * lax.rsqrt(ms + EPS) * sg_ref[...]) * (1.0 - lambda_init)).astype(o_ref.dtype)


def diff_attention(rot_q, rot_lat, proj_lat, rot_ctx, proj_ctx, lam_params, subln_g, lambda_init, tq, tk):
    mq = rot_q.shape[1]
    nq = mq // BATCH // tq
    with_lat = rot_lat is not None
    pair = lambda base, n: pl.BlockSpec((2, n, HEAD_DIM), lambda b, h, i: (base // 2 + h, b, 0))
    in_specs = [pl.BlockSpec((2, tq, HEAD_DIM), lambda b, h, i: (R_QB // 2 + h, b * nq + i, 0))]
    args = [rot_q]
    if with_lat:
        in_specs += [pair(R_KB, SEQ), pair(P_VB, SEQ)]
        args += [rot_lat, proj_lat]
    in_specs += [pair(R_KB, CTX_LEN), pair(P_VB, CTX_LEN)]
    args += [rot_ctx, proj_ctx]
    vec = lambda n: pl.BlockSpec((1, n), lambda b, h, i: (0, 0))
    in_specs += [vec(HEAD_DIM)] * 4 + [vec(2 * HEAD_DIM)]
    args += [p.reshape(1, HEAD_DIM) for p in lam_params] + [subln_g.reshape(1, 2 * HEAD_DIM)]
    scratch = [pltpu.VMEM((SEQ // tk, 2 * HEAD_DIM, tk), BF16)] if with_lat else []
    n_p = max(tk, CTX_LEN) if with_lat else CTX_LEN
    scratch += [pltpu.VMEM((2 * HEAD_DIM, CTX_LEN), BF16),
                pltpu.VMEM((2, 1, tq), F32), pltpu.VMEM((2, 1, tq), F32), pltpu.VMEM((2, 2 * HEAD_DIM, tq), F32),
                pltpu.VMEM((2, n_p, tq), BF16), pltpu.VMEM((2, 1, tq), F32)]
    return pl.pallas_call(
        functools.partial(_diff_kernel, with_lat=with_lat, tk=tk, lambda_init=lambda_init),
        grid=(BATCH, B_HEADS, nq),
        in_specs=in_specs,
        out_specs=pl.BlockSpec((tq, 2 * HEAD_DIM), lambda b, h, i: (b * nq + i, h)),
        out_shape=jax.ShapeDtypeStruct((mq, B_HEADS * 2 * HEAD_DIM), BF16),
        scratch_shapes=scratch,
        compiler_params=_params(("arbitrary", "arbitrary", "arbitrary")),
        name="diff_attention",
    )(*args)


NA_QROWS = 4
NA_CHUNK = NA_QROWS * GRID_W
NA_BAND_CHUNKS = 3
N_DR = 2 * NA_ROWS - 1
NA_ONLY_A, NA_ONLY_B, NA_NEITHER = N_DR - 1, 2 * N_DR - 1, 3 * N_DR - 1


def _na_tile_index(key_row, row_a):
    first = lambda r: jnp.clip(r - NA_ROWS // 2, 0, ROWS - NA_ROWS)
    sees = lambda r: jnp.logical_and(key_row >= first(r), key_row < first(r) + NA_ROWS)
    va, vb = sees(row_a), sees(row_a + 1)
    da = key_row - row_a + (NA_ROWS - 1)
    return jnp.where(jnp.logical_and(va, vb), da - 1,
                     jnp.where(va, NA_ONLY_A + da, jnp.where(vb, NA_ONLY_B - 1 + da, NA_NEITHER)))


def _na_kernel(q_ref, k_ref, v_ref, kc_ref, vc_ref, tab_ref, o_ref, vt_ref, vct_ref, p_ref, linv_ref):
    n_chunks = vt_ref.shape[0]
    chunk = lambda c: pl.ds(pl.multiple_of(c * NA_CHUNK, NA_CHUNK), NA_CHUNK)
    band_start = lambda blk: jnp.clip(blk - 1, 0, n_chunks - NA_BAND_CHUNKS)

    _transpose_to(vct_ref, vc_ref[...])

    def transpose_chunk(c, carry):
        _transpose_to(vt_ref.at[c], v_ref[chunk(c), :])
        return carry
    lax.fori_loop(0, n_chunks, transpose_chunk, 0)

    def to_probs(blk):
        row0 = blk * NA_QROWS
        cb = band_start(blk)
        q = q_ref[chunk(blk), :]
        s = []
        for c in range(NA_BAND_CHUNKS):
            tiles = []
            for i in range(NA_QROWS):
                key_row = (cb + c) * NA_QROWS + i
                tiles.append(jnp.concatenate(
                    [tab_ref[_na_tile_index(key_row, row0 + 2 * j)] for j in range(NA_QROWS // 2)], axis=1))
            s.append(_nt_dot(k_ref[chunk(cb + c), :], q) * SCALE_LOG2 + jnp.concatenate(tiles, axis=0))
        s.append(_nt_dot(kc_ref[...], q) * SCALE_LOG2)
        m = functools.reduce(jnp.maximum, [jnp.max(x, axis=0, keepdims=True) for x in s])
        l = None
        for c, x in enumerate(s):
            p = jnp.exp2(x - m)
            ps = jnp.sum(p, axis=0, keepdims=True)
            l = ps if l is None else l + ps
            p_ref[c] = p.astype(BF16)
        linv_ref[...] = 1.0 / l

    def to_output(blk):
        cb = band_start(blk)
        o_t = jnp.dot(vct_ref[...], p_ref[NA_BAND_CHUNKS], preferred_element_type=F32)
        for c in range(NA_BAND_CHUNKS):
            o_t = o_t + jnp.dot(vt_ref[cb + c], p_ref[c], preferred_element_type=F32)
        o_ref[chunk(blk), :] = (o_t * linv_ref[...]).T.astype(o_ref.dtype)

    to_probs(0)

    def block(blk, carry):
        to_output(blk - 1)
        to_probs(blk)
        return carry
    lax.fori_loop(1, n_chunks, block, 0)
    to_output(n_chunks - 1)


def neighbourhood_attention(proj_lat, proj_ctx, bias_table):
    assert CTX_LEN == NA_CHUNK
    lat = lambda base: pl.BlockSpec((None, SEQ, HEAD_DIM), lambda b, h: (base + h, b, 0))
    ctx = lambda base: pl.BlockSpec((None, CTX_LEN, HEAD_DIM), lambda b, h: (base + h, b, 0))
    return pl.pallas_call(
        _na_kernel,
        grid=(BATCH, C_HEADS),
        in_specs=[lat(0), lat(C_HEADS), lat(2 * C_HEADS), ctx(0), ctx(C_HEADS),
                  pl.BlockSpec((None, 3 * N_DR, GRID_W, 2 * GRID_W), lambda b, h: (h, 0, 0, 0))],
        out_specs=pl.BlockSpec((SEQ, HEAD_DIM), lambda b, h: (b, h)),
        out_shape=jax.ShapeDtypeStruct((BATCH * SEQ, C_HEADS * HEAD_DIM), BF16),
        scratch_shapes=[pltpu.VMEM((SEQ // NA_CHUNK, HEAD_DIM, NA_CHUNK), BF16),
                        pltpu.VMEM((HEAD_DIM, CTX_LEN), BF16),
                        pltpu.VMEM((NA_BAND_CHUNKS + 1, NA_CHUNK, NA_CHUNK), BF16),
                        pltpu.VMEM((1, NA_CHUNK), F32)],
        compiler_params=_params(("arbitrary", "arbitrary")),
        name="neighbourhood_attention",
    )(proj_lat, proj_lat, proj_lat, proj_ctx, proj_ctx, bias_table)


def _rope_tables():
    t = jnp.arange(SEQ, dtype=jnp.int32)
    row = (t // GRID_W).astype(F32)
    col = (t % GRID_W).astype(F32)
    n_freq = HEAD_DIM // 4
    freqs = ROPE_THETA ** (-jnp.arange(n_freq, dtype=F32) / n_freq)
    ar, ac = row[:, None] * freqs, col[:, None] * freqs
    cos_t = jnp.concatenate([jnp.cos(ar), jnp.cos(ar), jnp.cos(ac), jnp.cos(ac)], axis=1)
    sin_t = jnp.concatenate([-jnp.sin(ar), jnp.sin(ar), -jnp.sin(ac), jnp.sin(ac)], axis=1)
    return cos_t, sin_t


def _na_bias_table(rel_bias):
    col = jnp.arange(GRID_W, dtype=jnp.int32)
    c0 = jnp.clip(col - NA_COLS // 2, 0, GRID_W - NA_COLS)
    key, qry = col[:, None], col[None, :]
    dc = key - qry + (NA_COLS - 1)
    valid = (key >= c0[None, :]) & (key < c0[None, :] + NA_COLS)
    t = rel_bias[:, :, jnp.clip(dc, 0, 2 * NA_COLS - 2)].astype(F32) * LOG2E
    t = jnp.where(valid[None, None], t, NEG)
    off = jnp.full_like(t, NEG)
    both = jnp.concatenate([t[:, 1:], t[:, :-1]], axis=-1)
    only_a = jnp.concatenate([t, off], axis=-1)
    only_b = jnp.concatenate([off, t], axis=-1)
    neither = jnp.concatenate([off[:, :1], off[:, :1]], axis=-1)
    return jnp.concatenate([both, only_a, only_b, neither], axis=1)


def kernel(x, c, ctx, c_ctx, ada_w, ada_b, norm1_g, norm2_g, w_in_even, w_out_even, a_q_norm, a_k_norm,
           b_lambda_q1, b_lambda_k1, b_lambda_q2, b_lambda_k2, b_subln_g, w_in_odd, w_out_odd, na_rel_bias,
           mlp_w1, mlp_w2, final_g):
    TM = 1024
    TC = BATCH * CTX_LEN
    TN = 512
    lat_row = lambda m: m // (SEQ // TM)
    lat_row_n = lambda m: m // (SEQ // TN)
    ctx_row = lambda m: 2

    craw = jnp.concatenate([c, c_ctx[None], jnp.zeros((5, D_MODEL), F32)], axis=0)
    mod = ada_modulation(craw, ada_w, ada_b).reshape(DEPTH, 8, 6, 1, D_MODEL)

    h = x.reshape(BATCH * SEQ, D_MODEL)
    hc = ctx.reshape(TC, D_MODEL)
    cos_t, sin_t = _rope_tables()
    one_t, zero_t = jnp.ones((TC, HEAD_DIM), F32), jnp.zeros((TC, HEAD_DIM), F32)

    mod0 = mod[0]
    w_in = w_in_even[0].astype(BF16)
    w_out = w_out_even[0].astype(BF16)
    ones = jnp.ones((HEAD_DIM,), F32)
    gains = jnp.stack([a_q_norm[0] * SCALE_LOG2] * A_HEADS + [a_k_norm[0]] * A_KV_HEADS
                      + [ones * SCALE_LOG2] * (2 * B_HEADS) + [ones] * (2 * B_HEADS)).reshape(N_ROT, 1, HEAD_DIM)
    lam_params = (b_lambda_q1[0], b_lambda_k1[0], b_lambda_q2[0], b_lambda_k2[0])
    lambda_init = 0.8 - 0.6 * math.exp(-0.3 * 0)

    u = norm_modulate(h, norm1_g[0], mod0, 0, 1, lat_row_n, TN)
    uc = norm_modulate(hc, norm1_g[0], mod0, 0, 1, ctx_row, TC)
    proj = project_heads(u, w_in, TM, 1024)
    proj_c = project_heads(uc, w_in, TC, 1024)
    rot = prep_qk(proj, cos_t, sin_t, gains, 2048)
    rot_c = prep_qk(proj_c, one_t, zero_t, gains, TC)

    a_lat = gqa_attention(rot, proj, rot, proj, rot_c, proj_c, 128, 512)
    b_lat = diff_attention(rot, rot, proj, rot_c, proj_c, lam_params, b_subln_g[0], lambda_init, 512, 512)
    a_ctx = gqa_attention(rot_c, proj_c, None, None, rot_c, proj_c, 128, 512)
    b_ctx = diff_attention(rot_c, None, None, rot_c, proj_c, lam_params, b_subln_g[0], lambda_init, 256, 512)

    h = out_project_residual([a_lat, b_lat], w_out, h, mod0, 2, lat_row, TM, 1024)
    hc = out_project_residual([a_ctx, b_ctx], w_out, hc, mod0, 2, ctx_row, TC, 1024)
    w1 = mlp_w1[0].astype(BF16)
    w2 = mlp_w2[0].astype(BF16)
    h = mlp_residual(h, norm2_g[0], mod0, lat_row, w1, w2, final_g, False, TM, 256)
    hc = mlp_residual(hc, norm2_g[0], mod0, ctx_row, w1, w2, final_g, False, TC, 256)

    mod1 = mod[1]
    w_in = w_in_odd[0].astype(BF16)
    u = norm_modulate(h, norm1_g[1], mod1, 0, 1, lat_row_n, TN)
    uc = norm_modulate(hc, norm1_g[1], mod1, 0, 1, ctx_row, TC)
    proj = project_heads(u, w_in, TM, 1024)
    proj_c = project_heads(uc, w_in[:, C_HEADS * HEAD_DIM:], TC, 1024)
    attn = neighbourhood_attention(proj, proj_c, _na_bias_table(na_rel_bias[0]))
    h = out_project_residual([attn], w_out_odd[0].astype(BF16), h, mod1, 2, lat_row, TM, 1024)
    out = mlp_residual(h, norm2_g[1], mod1, lat_row, mlp_w1[1].astype(BF16), mlp_w2[1].astype(BF16),
                       final_g, True, TM, 256)
    return out.reshape(BATCH, SEQ, D_MODEL)
```

```python
import functools
import math

import jax
import jax.numpy as jnp
from jax import lax
from jax.experimental import pallas as pl
from jax.experimental.pallas import tpu as pltpu

D_MODEL = 4096
BATCH = 2
SEQ = 8192
DEPTH = 2
GRID_W = 64
ROWS = SEQ // GRID_W
CTX_LEN = 256
HEAD_DIM = 128
ROPE_THETA = 10000.0
EPS = 1e-6
A_HEADS = 16
A_KV_HEADS = 4
A_GROUP = A_HEADS // A_KV_HEADS
B_HEADS = 8
C_HEADS = 32
NA_ROWS = 8
NA_COLS = 16
MLP_HIDDEN = 4 * D_MODEL
EVEN_IN = 9216
SCALE = HEAD_DIM ** -0.5
LOG2E = 1.4426950408889634
SCALE_LOG2 = SCALE * LOG2E

P_QA, P_KA, P_VA, P_QB, P_KB, P_VB = 0, 16, 20, 24, 40, 56
N_PROJ_EVEN = EVEN_IN // HEAD_DIM
R_QA, R_KA, R_QB, R_KB = 0, 16, 20, 36
N_ROT = 52
N_NORMED = 20

LANES = 128
VMEM_LIMIT = 56 * 1024 * 1024
NEG = -1e30

F32 = jnp.float32
BF16 = jnp.bfloat16


def _params(sem, vmem=VMEM_LIMIT):
    return pltpu.CompilerParams(dimension_semantics=sem, vmem_limit_bytes=vmem)


def _ada_kernel(c_ref, w_ref, b_ref, o_ref):
    x = c_ref[...]
    cond = x / (1.0 + jnp.exp(-x))
    o_ref[...] = jnp.dot(cond.astype(BF16), w_ref[...].astype(BF16),
                         preferred_element_type=F32) + b_ref[...]


def ada_modulation(craw, ada_w, ada_b):
    tn = 512
    n6 = 6 * D_MODEL
    return pl.pallas_call(
        _ada_kernel,
        grid=(DEPTH, n6 // tn),
        in_specs=[pl.BlockSpec((8, D_MODEL), lambda i, n: (0, 0)),
                  pl.BlockSpec((None, D_MODEL, tn), lambda i, n: (i, 0, n)),
                  pl.BlockSpec((None, 1, tn), lambda i, n: (i, 0, n))],
        out_specs=pl.BlockSpec((None, 8, tn), lambda i, n: (i, 0, n)),
        out_shape=jax.ShapeDtypeStruct((DEPTH, 8, n6), F32),
        compiler_params=_params(("arbitrary", "arbitrary")),
        name="ada_modulation",
    )(craw, ada_w, ada_b.reshape(DEPTH, 1, n6))


ROW_CHUNK = 16


def _row_chunks(n_rows, fn):
    def body(i, carry):
        fn(pl.ds(pl.multiple_of(i * ROW_CHUNK, ROW_CHUNK), ROW_CHUNK))
        return carry
    lax.fori_loop(0, n_rows // ROW_CHUNK, body, 0)


def _modulate_into(dst_ref, h_ref, g_ref, sh_ref, sc_ref):
    g, up, sh = g_ref[...], 1.0 + sc_ref[...], sh_ref[...]

    def rows_fn(rows):
        x = h_ref[rows, :]
        ms = jnp.mean(x * x, axis=-1, keepdims=True)
        dst_ref[rows, :] = ((x * lax.rsqrt(ms + EPS) * g) * up + sh).astype(dst_ref.dtype)

    _row_chunks(h_ref.shape[0], rows_fn)


def _normmod_kernel(h_ref, g_ref, sh_ref, sc_ref, o_ref):
    _modulate_into(o_ref, h_ref, g_ref, sh_ref, sc_ref)


def _mod_spec(which, row_of):
    return pl.BlockSpec((None, None, 1, D_MODEL), lambda m, *_: (row_of(m), which, 0, 0))


def norm_modulate(h, g, mod, which_shift, which_scale, row_of, tm):
    m = h.shape[0]
    return pl.pallas_call(
        _normmod_kernel,
        grid=(m // tm,),
        in_specs=[pl.BlockSpec((tm, D_MODEL), lambda i: (i, 0)),
                  pl.BlockSpec((1, D_MODEL), lambda i: (0, 0)),
                  _mod_spec(which_shift, row_of), _mod_spec(which_scale, row_of)],
        out_specs=pl.BlockSpec((tm, D_MODEL), lambda i: (i, 0)),
        out_shape=jax.ShapeDtypeStruct((m, D_MODEL), BF16),
        compiler_params=_params(("arbitrary",)),
        name="norm_modulate",
    )(h, g.reshape(1, D_MODEL), mod, mod)


def _proj_kernel(x_ref, w_ref, o_ref):
    acc = jnp.dot(x_ref[...], w_ref[...], preferred_element_type=F32)
    for j in range(o_ref.shape[0]):
        o_ref[j] = acc[:, j * HEAD_DIM:(j + 1) * HEAD_DIM].astype(o_ref.dtype)


def project_heads(x, w, tm, tn):
    m, k = x.shape
    n = w.shape[1]
    return pl.pallas_call(
        _proj_kernel,
        grid=(n // tn, m // tm),
        in_specs=[pl.BlockSpec((tm, k), lambda j, i: (i, 0)),
                  pl.BlockSpec((k, tn), lambda j, i: (0, j))],
        out_specs=pl.BlockSpec((tn // HEAD_DIM, tm, HEAD_DIM), lambda j, i: (j, i, 0)),
        out_shape=jax.ShapeDtypeStruct((n // HEAD_DIM, m, HEAD_DIM), BF16),
        compiler_params=_params(("arbitrary", "arbitrary")),
        name="project_heads",
    )(x, w)


def _outproj_kernel(*refs, n_x):
    x_refs, (w_ref, h_ref, g_ref, o_ref) = refs[:n_x], refs[n_x:]
    acc = None
    off = 0
    for x_ref in x_refs:
        kk = x_ref.shape[1]
        part = jnp.dot(x_ref[...], w_ref[off:off + kk, :], preferred_element_type=F32)
        acc = part if acc is None else acc + part
        off += kk
    o_ref[...] = h_ref[...] + g_ref[...] * acc


def out_project_residual(xs, w, h, mod, which_gate, row_of, tm, tn):
    m = h.shape[0]
    k = w.shape[0]
    in_specs = [pl.BlockSpec((tm, x.shape[1]), lambda j, i: (i, 0)) for x in xs]
    in_specs += [pl.BlockSpec((k, tn), lambda j, i: (0, j)),
                 pl.BlockSpec((tm, tn), lambda j, i: (i, j)),
                 pl.BlockSpec((None, None, 1, tn), lambda j, i: (row_of(i), which_gate, 0, j))]
    return pl.pallas_call(
        functools.partial(_outproj_kernel, n_x=len(xs)),
        grid=(D_MODEL // tn, m // tm),
        in_specs=in_specs,
        out_specs=pl.BlockSpec((tm, tn), lambda j, i: (i, j)),
        out_shape=jax.ShapeDtypeStruct((m, D_MODEL), F32),
        compiler_params=_params(("arbitrary", "arbitrary")),
        name="out_project_residual",
    )(*xs, w, h, mod)


def _mlp_kernel(h_ref, gn_ref, sh_ref, sc_ref, gate_ref, w1_ref, w2_ref, fg_ref, o_ref, xs_ref, hid_ref, *,
                final_norm):
    j = pl.program_id(1)

    @pl.when(j == 0)
    def _():
        _modulate_into(xs_ref, h_ref, gn_ref, sh_ref, sc_ref)
        o_ref[...] = jnp.zeros_like(o_ref)
        hid_ref[...] = jnp.zeros_like(hid_ref)

    o_ref[...] = jnp.dot(hid_ref[...], w2_ref[...], preferred_element_type=F32) + o_ref[...]
    hid = jnp.dot(xs_ref[...], w1_ref[...], preferred_element_type=F32)
    hid_ref[...] = jnp.square(jnp.maximum(hid, 0.0)).astype(BF16)

    @pl.when(j == pl.num_programs(1) - 1)
    def _():
        gate, fg = gate_ref[...], fg_ref[...]

        def rows_fn(rows):
            y = h_ref[rows, :] + gate * o_ref[rows, :]
            if final_norm:
                ms = jnp.mean(y * y, axis=-1, keepdims=True)
                y = y * lax.rsqrt(ms + EPS) * fg
            o_ref[rows, :] = y

        _row_chunks(h_ref.shape[0], rows_fn)


def mlp_residual(h, gn, mod, row_of, w1, w2, final_g, final_norm, tm, th):
    m = h.shape[0]
    nj = MLP_HIDDEN // th
    once = pl.Buffered(1)
    row = lambda i, j: (i, 0)
    return pl.pallas_call(
        functools.partial(_mlp_kernel, final_norm=final_norm),
        grid=(m // tm, nj + 1),
        in_specs=[pl.BlockSpec((tm, D_MODEL), row, pipeline_mode=once),
                  pl.BlockSpec((1, D_MODEL), lambda i, j: (0, 0)),
                  _mod_spec(3, row_of), _mod_spec(4, row_of), _mod_spec(5, row_of),
                  pl.BlockSpec((None, D_MODEL, th), lambda i, j: (jnp.minimum(j, nj - 1), 0, 0)),
                  pl.BlockSpec((th, D_MODEL), lambda i, j: (jnp.maximum(j - 1, 0), 0)),
                  pl.BlockSpec((1, D_MODEL), lambda i, j: (0, 0))],
        out_specs=pl.BlockSpec((tm, D_MODEL), row, pipeline_mode=once),
        out_shape=jax.ShapeDtypeStruct((m, D_MODEL), F32),
        scratch_shapes=[pltpu.VMEM((tm, D_MODEL), BF16), pltpu.VMEM((tm, th), BF16)],
        compiler_params=_params(("arbitrary", "arbitrary")),
        name="mlp_residual",
    )(h, gn.reshape(1, D_MODEL), mod, mod, mod,
      w1.reshape(D_MODEL, nj, th).transpose(1, 0, 2),
      w2, final_g.reshape(1, D_MODEL))


def _prep_kernel(x_ref, c_ref, s_ref, g_ref, o_ref):
    j = pl.program_id(1)
    x = x_ref[...].astype(F32)
    ms = jnp.mean(x * x, axis=-1, keepdims=True)
    inv = jnp.where(j < N_NORMED, lax.rsqrt(ms + EPS), 1.0)
    y = x * inv * g_ref[...]
    lane = lax.broadcasted_iota(jnp.int32, y.shape, 1)
    partner = jnp.where(lane % 64 < 32, pltpu.roll(y, LANES - 32, 1), pltpu.roll(y, 32, 1))
    o_ref[...] = (y * c_ref[...] + partner * s_ref[...]).astype(o_ref.dtype)


def prep_qk(proj, cos_t, sin_t, gains, tm):
    m = proj.shape[1]
    nt = cos_t.shape[0] // tm
    src = lambda j: jnp.where(j < N_NORMED, j, j + (P_QB - N_NORMED))
    return pl.pallas_call(
        _prep_kernel,
        grid=(m // tm, N_ROT),
        in_specs=[pl.BlockSpec((None, tm, HEAD_DIM), lambda i, j: (src(j), i, 0)),
                  pl.BlockSpec((tm, HEAD_DIM), lambda i, j: (i % nt, 0)),
                  pl.BlockSpec((tm, HEAD_DIM), lambda i, j: (i % nt, 0)),
                  pl.BlockSpec((None, 1, HEAD_DIM), lambda i, j: (j, 0, 0))],
        out_specs=pl.BlockSpec((None, tm, HEAD_DIM), lambda i, j: (j, i, 0)),
        out_shape=jax.ShapeDtypeStruct((N_ROT, m, HEAD_DIM), BF16),
        compiler_params=_params(("arbitrary", "arbitrary")),
        name="prep_qk",
    )(proj, cos_t, sin_t, gains)


def _nt_dot(a, b):
    return lax.dot_general(a, b, (((1,), (1,)), ((), ())), preferred_element_type=F32)


def _transpose_to(dst_ref, src):
    dst_ref[...] = src.astype(F32).T.astype(dst_ref.dtype)


def _pipeline3(n, stage1, stage2, stage3, last=None):
    assert n % 2 == 0 and n >= 2
    stage1(0, 0)
    stage1(1, 1)
    stage2(0, 0)

    def pair(t, carry):
        i = 2 * t + 1
        stage3(i - 1, 0)
        stage1(i + 1, 0)
        stage2(i, 1)
        stage3(i, 1)
        stage1(i + 2, 1)
        stage2(i + 1, 0)
        return carry
    lax.fori_loop(0, (n - 2) // 2, pair, 0)
    stage3(n - 2, 0)
    if last is not None:
        last[0](0)
    stage2(n - 1, 1)
    stage3(n - 1, 1)
    if last is not None:
        last[1](0)
        last[2](0)


def _scores(q, k, s_ref):
    s_ref[...] = _nt_dot(k, q)


def _probs(s_ref, m_ref, l_ref, p_ref, alpha_ref):
    s = s_ref[...]
    m_prev = m_ref[...]
    m_new = jnp.maximum(m_prev, jnp.max(s, axis=0, keepdims=True))
    alpha = jnp.exp2(m_prev - m_new)
    p = jnp.exp2(s - m_new)
    l_ref[...] = alpha * l_ref[...] + jnp.sum(p, axis=0, keepdims=True)
    m_ref[...] = m_new
    p_ref[...] = p.astype(BF16)
    alpha_ref[...] = alpha


def _probs_to_acc(vt, p_ref, alpha_ref, acc_ref):
    acc_ref[...] = alpha_ref[...] * acc_ref[...] + jnp.dot(vt, p_ref[...], preferred_element_type=F32)


def _gqa_kernel(*refs, with_lat, tk):
    if with_lat:
        (q_ref, kl_ref, vl_ref, kc_ref, vc_ref, o_ref,
         vlt_ref, vct_ref, m_ref, l_ref, acc_ref, s_ref, p_ref, alpha_ref) = refs
    else:
        q_ref, kc_ref, vc_ref, o_ref, vct_ref, m_ref, l_ref, acc_ref, s_ref, p_ref, alpha_ref = refs
    g, tq, d = q_ref.shape
    n_ctx = kc_ref.shape[0]

    @pl.when(pl.program_id(2) == 0)
    def _():
        _transpose_to(vct_ref, vc_ref[...])
        if with_lat:
            def body(c, carry):
                _transpose_to(vlt_ref.at[c], vl_ref[pl.ds(pl.multiple_of(c * tk, tk), tk), :])
                return carry
            lax.fori_loop(0, vlt_ref.shape[0], body, 0)

    q = q_ref[...].reshape(g * tq, d)
    m_ref[...] = jnp.full_like(m_ref, NEG)
    l_ref[...] = jnp.zeros_like(l_ref)
    acc_ref[...] = jnp.zeros_like(acc_ref)
    ctx_stages = (
        lambda slot: _scores(q, kc_ref[...], s_ref.at[slot, 0:n_ctx, :]),
        lambda slot: _probs(s_ref.at[slot, 0:n_ctx, :], m_ref, l_ref, p_ref.at[slot, 0:n_ctx, :], alpha_ref.at[slot]),
        lambda slot: _probs_to_acc(vct_ref[...], p_ref.at[slot, 0:n_ctx, :], alpha_ref.at[slot], acc_ref))
    if with_lat:
        rows = lambda c: pl.ds(pl.multiple_of(c * tk, tk), tk)
        _pipeline3(
            vlt_ref.shape[0],
            lambda c, slot: _scores(q, kl_ref[rows(c), :], s_ref.at[slot]),
            lambda c, slot: _probs(s_ref.at[slot], m_ref, l_ref, p_ref.at[slot], alpha_ref.at[slot]),
            lambda c, slot: _probs_to_acc(vlt_ref[c], p_ref.at[slot], alpha_ref.at[slot], acc_ref),
            last=ctx_stages)
    else:
        for stage in ctx_stages:
            stage(0)
    o = (acc_ref[...] * (1.0 / l_ref[...])).T
    for h in range(g):
        o_ref[:, h * d:(h + 1) * d] = o[h * tq:(h + 1) * tq].astype(o_ref.dtype)


def gqa_attention(rot_q, proj_q, rot_lat, proj_lat, rot_ctx, proj_ctx, tq, tk):
    mq = rot_q.shape[1]
    nq = mq // BATCH // tq
    with_lat = rot_lat is not None
    q_spec = pl.BlockSpec((A_GROUP, tq, HEAD_DIM), lambda b, g, i: (g, b * nq + i, 0))
    kv = lambda base, n: pl.BlockSpec((None, n, HEAD_DIM), lambda b, g, i: (base + g, b, 0))
    in_specs, args = [q_spec], [rot_q]
    if with_lat:
        in_specs += [kv(R_KA, SEQ), kv(P_VA, SEQ)]
        args += [rot_lat, proj_lat]
    in_specs += [kv(R_KA, CTX_LEN), kv(P_VA, CTX_LEN)]
    args += [rot_ctx, proj_ctx]
    rows = A_GROUP * tq
    scratch = [pltpu.VMEM((SEQ // tk, HEAD_DIM, tk), BF16)] if with_lat else []
    n_p = max(tk, CTX_LEN) if with_lat else CTX_LEN
    scratch += [pltpu.VMEM((HEAD_DIM, CTX_LEN), BF16),
                pltpu.VMEM((1, rows), F32), pltpu.VMEM((1, rows), F32), pltpu.VMEM((HEAD_DIM, rows), F32),
                pltpu.VMEM((2, n_p, rows), F32), pltpu.VMEM((2, n_p, rows), BF16), pltpu.VMEM((2, 1, rows), F32)]
    return pl.pallas_call(
        functools.partial(_gqa_kernel, with_lat=with_lat, tk=tk),
        grid=(BATCH, A_KV_HEADS, nq),
        in_specs=in_specs,
        out_specs=pl.BlockSpec((tq, A_GROUP * HEAD_DIM), lambda b, g, i: (b * nq + i, g)),
        out_shape=jax.ShapeDtypeStruct((mq, A_HEADS * HEAD_DIM), BF16),
        scratch_shapes=scratch,
        compiler_params=_params(("arbitrary", "arbitrary", "arbitrary")),
        name="gqa_attention",
    )(*args)


def _diff_kernel(*refs, with_lat, tk, lambda_init):
    if with_lat:
        (q_ref, kl_ref, vl_ref, kc_ref, vc_ref, lq1_ref, lk1_ref, lq2_ref, lk2_ref, sg_ref,
         o_ref, vlt_ref, vct_ref, m_ref, l_ref, acc_ref, s_ref, p_ref, alpha_ref) = refs
    else:
        (q_ref, kc_ref, vc_ref, lq1_ref, lk1_ref, lq2_ref, lk2_ref, sg_ref,
         o_ref, vct_ref, m_ref, l_ref, acc_ref, s_ref, p_ref, alpha_ref) = refs
    d = q_ref.shape[2]

    @pl.when(pl.program_id(2) == 0)
    def _():
        for half in range(2):
            _transpose_to(vct_ref.at[half * d:(half + 1) * d, :], vc_ref[half])
        if with_lat:
            def body(c, carry):
                rows = pl.ds(pl.multiple_of(c * tk, tk), tk)
                for half in range(2):
                    _transpose_to(vlt_ref.at[c, half * d:(half + 1) * d, :], vl_ref[half, rows, :])
                return carry
            lax.fori_loop(0, vlt_ref.shape[0], body, 0)

    m_ref[...] = jnp.full_like(m_ref, NEG)
    l_ref[...] = jnp.zeros_like(l_ref)
    acc_ref[...] = jnp.zeros_like(acc_ref)

    def scores(k_of, s_of):
        for i in range(2):
            _scores(q_ref[i], k_of(i), s_of(i))

    def probs(s_of, p_of, slot):
        for i in range(2):
            _probs(s_of(i), m_ref.at[i], l_ref.at[i], p_of(i), alpha_ref.at[slot, i])

    def to_acc(vt, p_of, slot):
        for i in range(2):
            _probs_to_acc(vt, p_of(i), alpha_ref.at[slot, i], acc_ref.at[i])

    n_ctx = kc_ref.shape[1]
    s_ctx = lambda slot: lambda i: s_ref.at[slot, i, 0:n_ctx, :]
    p_ctx = lambda slot: lambda i: p_ref.at[slot, i, 0:n_ctx, :]
    ctx_stages = (
        lambda slot: scores(lambda i: kc_ref[i], s_ctx(slot)),
        lambda slot: probs(s_ctx(slot), p_ctx(slot), slot),
        lambda slot: to_acc(vct_ref[...], p_ctx(slot), slot))
    if with_lat:
        rows = lambda c: pl.ds(pl.multiple_of(c * tk, tk), tk)
        _pipeline3(
            vlt_ref.shape[0],
            lambda c, slot: scores(lambda i: kl_ref[i, rows(c), :], lambda i: s_ref.at[slot, i]),
            lambda c, slot: probs(lambda i: s_ref.at[slot, i], lambda i: p_ref.at[slot, i], slot),
            lambda c, slot: to_acc(vlt_ref[c], lambda i: p_ref.at[slot, i], slot),
            last=ctx_stages)
    else:
        for stage in ctx_stages:
            stage(0)

    lam = (jnp.exp(jnp.sum(lq1_ref[...] * lk1_ref[...], axis=1, keepdims=True))
           - jnp.exp(jnp.sum(lq2_ref[...] * lk2_ref[...], axis=1, keepdims=True)) + lambda_init)
    o = (acc_ref[0] * (1.0 / l_ref[0]) - lam * (acc_ref[1] * (1.0 / l_ref[1]))).T
    ms = jnp.mean(o * o, axis=-1, keepdims=True)
    o_ref[...] = ((o * lax.rsqrt(ms + EPS) * sg_ref[...]) * (1.0 - lambda_init)).astype(o_ref.dtype)


def diff_attention(rot_q, rot_lat, proj_lat, rot_ctx, proj_ctx, lam_params, subln_g, lambda_init, tq, tk):
    mq = rot_q.shape[1]
    nq = mq // BATCH // tq
    with_lat = rot_lat is not None
    pair = lambda base, n: pl.BlockSpec((2, n, HEAD_DIM), lambda b, h, i: (base // 2 + h, b, 0))
    in_specs = [pl.BlockSpec((2, tq, HEAD_DIM), lambda b, h, i: (R_QB // 2 + h, b * nq + i, 0))]
    args = [rot_q]
    if with_lat:
        in_specs += [pair(R_KB, SEQ), pair(P_VB, SEQ)]
        args += [rot_lat, proj_lat]
    in_specs += [pair(R_KB, CTX_LEN), pair(P_VB, CTX_LEN)]
    args += [rot_ctx, proj_ctx]
    vec = lambda n: pl.BlockSpec((1, n), lambda b, h, i: (0, 0))
    in_specs += [vec(HEAD_DIM)] * 4 + [vec(2 * HEAD_DIM)]
    args += [p.reshape(1, HEAD_DIM) for p in lam_params] + [subln_g.reshape(1, 2 * HEAD_DIM)]
    scratch = [pltpu.VMEM((SEQ // tk, 2 * HEAD_DIM, tk), BF16)] if with_lat else []
    n_p = max(tk, CTX_LEN) if with_lat else CTX_LEN
    scratch += [pltpu.VMEM((2 * HEAD_DIM, CTX_LEN), BF16),
                pltpu.VMEM((2, 1, tq), F32), pltpu.VMEM((2, 1, tq), F32), pltpu.VMEM((2, 2 * HEAD_DIM, tq), F32),
                pltpu.VMEM((2, 2, n_p, tq), F32), pltpu.VMEM((2, 2, n_p, tq), BF16),
                pltpu.VMEM((2, 2, 1, tq), F32)]
    return pl.pallas_call(
        functools.partial(_diff_kernel, with_lat=with_lat, tk=tk, lambda_init=lambda_init),
        grid=(BATCH, B_HEADS, nq),
        in_specs=in_specs,
        out_specs=pl.BlockSpec((tq, 2 * HEAD_DIM), lambda b, h, i: (b * nq + i, h)),
        out_shape=jax.ShapeDtypeStruct((mq, B_HEADS * 2 * HEAD_DIM), BF16),
        scratch_shapes=scratch,
        compiler_params=_params(("arbitrary", "arbitrary", "arbitrary")),
        name="diff_attention",
    )(*args)


NA_QROWS = 4
NA_CHUNK = NA_QROWS * GRID_W
NA_BAND_CHUNKS = 3
N_DR = 2 * NA_ROWS - 1
NA_ONLY_A, NA_ONLY_B, NA_NEITHER = N_DR - 1, 2 * N_DR - 1, 3 * N_DR - 1


def _na_tile_index(key_row, row_a):
    first = lambda r: jnp.clip(r - NA_ROWS // 2, 0, ROWS - NA_ROWS)
    sees = lambda r: jnp.logical_and(key_row >= first(r), key_row < first(r) + NA_ROWS)
    va, vb = sees(row_a), sees(row_a + 1)
    da = key_row - row_a + (NA_ROWS - 1)
    return jnp.where(jnp.logical_and(va, vb), da - 1,
                     jnp.where(va, NA_ONLY_A + da, jnp.where(vb, NA_ONLY_B - 1 + da, NA_NEITHER)))


def _na_kernel(q_ref, k_ref, v_ref, kc_ref, vc_ref, tab_ref, o_ref, vt_ref, vct_ref, s_ref, p_ref, linv_ref):
    n_chunks = vt_ref.shape[0]
    chunk = lambda c: pl.ds(pl.multiple_of(c * NA_CHUNK, NA_CHUNK), NA_CHUNK)
    band_start = lambda blk: jnp.clip(blk - 1, 0, n_chunks - NA_BAND_CHUNKS)

    _transpose_to(vct_ref, vc_ref[...])

    def transpose_chunk(c, carry):
        _transpose_to(vt_ref.at[c], v_ref[chunk(c), :])
        return carry
    lax.fori_loop(0, n_chunks, transpose_chunk, 0)

    def scores(blk, slot):
        row0 = blk * NA_QROWS
        cb = band_start(blk)
        q = q_ref[chunk(blk), :]
        for c in range(NA_BAND_CHUNKS):
            tiles = []
            for i in range(NA_QROWS):
                key_row = (cb + c) * NA_QROWS + i
                tiles.append(jnp.concatenate(
                    [tab_ref[_na_tile_index(key_row, row0 + 2 * j)] for j in range(NA_QROWS // 2)], axis=1))
            s_ref[slot, c] = _nt_dot(k_ref[chunk(cb + c), :], q) * SCALE_LOG2 + jnp.concatenate(tiles, axis=0)
        s_ref[slot, NA_BAND_CHUNKS] = _nt_dot(kc_ref[...], q) * SCALE_LOG2

    def probs(blk, slot):
        n = NA_BAND_CHUNKS + 1
        m = functools.reduce(jnp.maximum, [jnp.max(s_ref[slot, c], axis=0, keepdims=True) for c in range(n)])
        l = None
        for c in range(n):
            p = jnp.exp2(s_ref[slot, c] - m)
            ps = jnp.sum(p, axis=0, keepdims=True)
            l = ps if l is None else l + ps
            p_ref[slot, c] = p.astype(BF16)
        linv_ref[slot] = 1.0 / l

    def output(blk, slot):
        cb = band_start(blk)
        o_t = jnp.dot(vct_ref[...], p_ref[slot, NA_BAND_CHUNKS], preferred_element_type=F32)
        for c in range(NA_BAND_CHUNKS):
            o_t = o_t + jnp.dot(vt_ref[cb + c], p_ref[slot, c], preferred_element_type=F32)
        o_ref[chunk(blk), :] = (o_t * linv_ref[slot]).T.astype(o_ref.dtype)

    _pipeline3(n_chunks, scores, probs, output)


def neighbourhood_attention(proj_lat, proj_ctx, bias_table):
    assert CTX_LEN == NA_CHUNK
    lat = lambda base: pl.BlockSpec((None, SEQ, HEAD_DIM), lambda b, h: (base + h, b, 0))
    ctx = lambda base: pl.BlockSpec((None, CTX_LEN, HEAD_DIM), lambda b, h: (base + h, b, 0))
    return pl.pallas_call(
        _na_kernel,
        grid=(BATCH, C_HEADS),
        in_specs=[lat(0), lat(C_HEADS), lat(2 * C_HEADS), ctx(0), ctx(C_HEADS),
                  pl.BlockSpec((None, 3 * N_DR, GRID_W, 2 * GRID_W), lambda b, h: (h, 0, 0, 0))],
        out_specs=pl.BlockSpec((SEQ, HEAD_DIM), lambda b, h: (b, h)),
        out_shape=jax.ShapeDtypeStruct((BATCH * SEQ, C_HEADS * HEAD_DIM), BF16),
        scratch_shapes=[pltpu.VMEM((SEQ // NA_CHUNK, HEAD_DIM, NA_CHUNK), BF16),
                        pltpu.VMEM((HEAD_DIM, CTX_LEN), BF16),
                        pltpu.VMEM((2, NA_BAND_CHUNKS + 1, NA_CHUNK, NA_CHUNK), F32),
                        pltpu.VMEM((2, NA_BAND_CHUNKS + 1, NA_CHUNK, NA_CHUNK), BF16),
                        pltpu.VMEM((2, 1, NA_CHUNK), F32)],
        compiler_params=_params(("arbitrary", "arbitrary")),
        name="neighbourhood_attention",
    )(proj_lat, proj_lat, proj_lat, proj_ctx, proj_ctx, bias_table)


def _rope_tables():
    t = jnp.arange(SEQ, dtype=jnp.int32)
    row = (t // GRID_W).astype(F32)
    col = (t % GRID_W).astype(F32)
    n_freq = HEAD_DIM // 4
    freqs = ROPE_THETA ** (-jnp.arange(n_freq, dtype=F32) / n_freq)
    ar, ac = row[:, None] * freqs, col[:, None] * freqs
    cos_t = jnp.concatenate([jnp.cos(ar), jnp.cos(ar), jnp.cos(ac), jnp.cos(ac)], axis=1)
    sin_t = jnp.concatenate([-jnp.sin(ar), jnp.sin(ar), -jnp.sin(ac), jnp.sin(ac)], axis=1)
    return cos_t, sin_t


def _na_bias_table(rel_bias):
    col = jnp.arange(GRID_W, dtype=jnp.int32)
    c0 = jnp.clip(col - NA_COLS // 2, 0, GRID_W - NA_COLS)
    key, qry = col[:, None], col[None, :]
    dc = key - qry + (NA_COLS - 1)
    valid = (key >= c0[None, :]) & (key < c0[None, :] + NA_COLS)
    t = rel_bias[:, :, jnp.clip(dc, 0, 2 * NA_COLS - 2)].astype(F32) * LOG2E
    t = jnp.where(valid[None, None], t, NEG)
    off = jnp.full_like(t, NEG)
    both = jnp.concatenate([t[:, 1:], t[:, :-1]], axis=-1)
    only_a = jnp.concatenate([t, off], axis=-1)
    only_b = jnp.concatenate([off, t], axis=-1)
    neither = jnp.concatenate([off[:, :1], off[:, :1]], axis=-1)
    return jnp.concatenate([both, only_a, only_b, neither], axis=1)


def kernel(x, c, ctx, c_ctx, ada_w, ada_b, norm1_g, norm2_g, w_in_even, w_out_even, a_q_norm, a_k_norm,
           b_lambda_q1, b_lambda_k1, b_lambda_q2, b_lambda_k2, b_subln_g, w_in_odd, w_out_odd, na_rel_bias,
           mlp_w1, mlp_w2, final_g):
    TM = 1024
    TC = BATCH * CTX_LEN
    TN = 512
    lat_row = lambda m: m // (SEQ // TM)
    lat_row_n = lambda m: m // (SEQ // TN)
    ctx_row = lambda m: 2

    craw = jnp.concatenate([c, c_ctx[None], jnp.zeros((5, D_MODEL), F32)], axis=0)
    mod = ada_modulation(craw, ada_w, ada_b).reshape(DEPTH, 8, 6, 1, D_MODEL)

    h = x.reshape(BATCH * SEQ, D_MODEL)
    hc = ctx.reshape(TC, D_MODEL)
    cos_t, sin_t = _rope_tables()
    one_t, zero_t = jnp.ones((TC, HEAD_DIM), F32), jnp.zeros((TC, HEAD_DIM), F32)

    mod0 = mod[0]
    w_in = w_in_even[0].astype(BF16)
    w_out = w_out_even[0].astype(BF16)
    ones = jnp.ones((HEAD_DIM,), F32)
    gains = jnp.stack([a_q_norm[0] * SCALE_LOG2] * A_HEADS + [a_k_norm[0]] * A_KV_HEADS
                      + [ones * SCALE_LOG2] * (2 * B_HEADS) + [ones] * (2 * B_HEADS)).reshape(N_ROT, 1, HEAD_DIM)
    lam_params = (b_lambda_q1[0], b_lambda_k1[0], b_lambda_q2[0], b_lambda_k2[0])
    lambda_init = 0.8 - 0.6 * math.exp(-0.3 * 0)

    u = norm_modulate(h, norm1_g[0], mod0, 0, 1, lat_row_n, TN)
    uc = norm_modulate(hc, norm1_g[0], mod0, 0, 1, ctx_row, TC)
    proj = project_heads(u, w_in, TM, 1024)
    proj_c = project_heads(uc, w_in, TC, 1024)
    rot = prep_qk(proj, cos_t, sin_t, gains, 2048)
    rot_c = prep_qk(proj_c, one_t, zero_t, gains, TC)

    a_lat = gqa_attention(rot, proj, rot, proj, rot_c, proj_c, 128, 512)
    b_lat = diff_attention(rot, rot, proj, rot_c, proj_c, lam_params, b_subln_g[0], lambda_init, 512, 512)
    a_ctx = gqa_attention(rot_c, proj_c, None, None, rot_c, proj_c, 128, 512)
    b_ctx = diff_attention(rot_c, None, None, rot_c, proj_c, lam_params, b_subln_g[0], lambda_init, 256, 512)

    h = out_project_residual([a_lat, b_lat], w_out, h, mod0, 2, lat_row, TM, 1024)
    hc = out_project_residual([a_ctx, b_ctx], w_out, hc, mod0, 2, ctx_row, TC, 1024)
    w1 = mlp_w1[0].astype(BF16)
    w2 = mlp_w2[0].astype(BF16)
    h = mlp_residual(h, norm2_g[0], mod0, lat_row, w1, w2, final_g, False, TM, 256)
    hc = mlp_residual(hc, norm2_g[0], mod0, ctx_row, w1, w2, final_g, False, TC, 256)

    mod1 = mod[1]
    w_in = w_in_odd[0].astype(BF16)
    u = norm_modulate(h, norm1_g[1], mod1, 0, 1, lat_row_n, TN)
    uc = norm_modulate(hc, norm1_g[1], mod1, 0, 1, ctx_row, TC)
    proj = project_heads(u, w_in, TM, 1024)
    proj_c = project_heads(uc, w_in[:, C_HEADS * HEAD_DIM:], TC, 1024)
    attn = neighbourhood_attention(proj, proj_c, _na_bias_table(na_rel_bias[0]))
    h = out_project_residual([attn], w_out_odd[0].astype(BF16), h, mod1, 2, lat_row, TM, 1024)
    out = mlp_residual(h, norm2_g[1], mod1, lat_row, mlp_w1[1].astype(BF16), mlp_w2[1].astype(BF16),
                       final_g, True, TM, 256)
    return out.reshape(BATCH, SEQ, D_MODEL)
```

```python
import functools
import math

import jax
import jax.numpy as jnp
from jax import lax
from jax.experimental import pallas as pl
from jax.experimental.pallas import tpu as pltpu

D_MODEL = 4096
BATCH = 2
SEQ = 8192
DEPTH = 2
GRID_W = 64
ROWS = SEQ // GRID_W
CTX_LEN = 256
HEAD_DIM = 128
ROPE_THETA = 10000.0
EPS = 1e-6
A_HEADS = 16
A_KV_HEADS = 4
A_GROUP = A_HEADS // A_KV_HEADS
B_HEADS = 8
C_HEADS = 32
NA_ROWS = 8
NA_COLS = 16
MLP_HIDDEN = 4 * D_MODEL
EVEN_IN = 9216
SCALE = HEAD_DIM ** -0.5
LOG2E = 1.4426950408889634
SCALE_LOG2 = SCALE * LOG2E

P_QA, P_KA, P_VA, P_QB, P_KB, P_VB = 0, 16, 20, 24, 40, 56
N_PROJ_EVEN = EVEN_IN // HEAD_DIM
R_QA, R_KA, R_QB, R_KB = 0, 16, 20, 36
N_ROT = 52
N_NORMED = 20

LANES = 128
VMEM_LIMIT = 62 * 1024 * 1024
NEG = -1e30

F32 = jnp.float32
BF16 = jnp.bfloat16


def _params(sem, vmem=VMEM_LIMIT):
    return pltpu.CompilerParams(dimension_semantics=sem, vmem_limit_bytes=vmem)


def _ada_kernel(c_ref, w_ref, b_ref, o_ref):
    x = c_ref[...]
    cond = x / (1.0 + jnp.exp(-x))
    o_ref[...] = jnp.dot(cond.astype(BF16), w_ref[...].astype(BF16),
                         preferred_element_type=F32) + b_ref[...]


def ada_modulation(craw, ada_w, ada_b):
    tn = 512
    n6 = 6 * D_MODEL
    return pl.pallas_call(
        _ada_kernel,
        grid=(DEPTH, n6 // tn),
        in_specs=[pl.BlockSpec((8, D_MODEL), lambda i, n: (0, 0)),
                  pl.BlockSpec((None, D_MODEL, tn), lambda i, n: (i, 0, n)),
                  pl.BlockSpec((None, 1, tn), lambda i, n: (i, 0, n))],
        out_specs=pl.BlockSpec((None, 8, tn), lambda i, n: (i, 0, n)),
        out_shape=jax.ShapeDtypeStruct((DEPTH, 8, n6), F32),
        compiler_params=_params(("arbitrary", "arbitrary")),
        name="ada_modulation",
    )(craw, ada_w, ada_b.reshape(DEPTH, 1, n6))


ROW_CHUNK = 16


def _row_chunks(n_rows, fn):
    def body(i, carry):
        fn(pl.ds(pl.multiple_of(i * ROW_CHUNK, ROW_CHUNK), ROW_CHUNK))
        return carry
    lax.fori_loop(0, n_rows // ROW_CHUNK, body, 0)


def _modulate_into(dst_ref, h_ref, g_ref, sh_ref, sc_ref):
    g, up, sh = g_ref[...], 1.0 + sc_ref[...], sh_ref[...]

    def rows_fn(rows):
        x = h_ref[rows, :]
        ms = jnp.mean(x * x, axis=-1, keepdims=True)
        dst_ref[rows, :] = ((x * lax.rsqrt(ms + EPS) * g) * up + sh).astype(dst_ref.dtype)

    _row_chunks(h_ref.shape[0], rows_fn)


def _normmod_kernel(h_ref, g_ref, sh_ref, sc_ref, o_ref):
    _modulate_into(o_ref, h_ref, g_ref, sh_ref, sc_ref)


def _mod_spec(which, row_of):
    return pl.BlockSpec((None, None, 1, D_MODEL), lambda m, *_: (row_of(m), which, 0, 0))


def norm_modulate(h, g, mod, which_shift, which_scale, row_of, tm):
    m = h.shape[0]
    return pl.pallas_call(
        _normmod_kernel,
        grid=(m // tm,),
        in_specs=[pl.BlockSpec((tm, D_MODEL), lambda i: (i, 0)),
                  pl.BlockSpec((1, D_MODEL), lambda i: (0, 0)),
                  _mod_spec(which_shift, row_of), _mod_spec(which_scale, row_of)],
        out_specs=pl.BlockSpec((tm, D_MODEL), lambda i: (i, 0)),
        out_shape=jax.ShapeDtypeStruct((m, D_MODEL), BF16),
        compiler_params=_params(("arbitrary",)),
        name="norm_modulate",
    )(h, g.reshape(1, D_MODEL), mod, mod)


def _proj_kernel(x_ref, w_ref, o_ref):
    acc = jnp.dot(x_ref[...], w_ref[...], preferred_element_type=F32)
    for j in range(o_ref.shape[0]):
        o_ref[j] = acc[:, j * HEAD_DIM:(j + 1) * HEAD_DIM].astype(o_ref.dtype)


def project_heads(x, w, tm, tn):
    m, k = x.shape
    n = w.shape[1]
    return pl.pallas_call(
        _proj_kernel,
        grid=(n // tn, m // tm),
        in_specs=[pl.BlockSpec((tm, k), lambda j, i: (i, 0)),
                  pl.BlockSpec((k, tn), lambda j, i: (0, j))],
        out_specs=pl.BlockSpec((tn // HEAD_DIM, tm, HEAD_DIM), lambda j, i: (j, i, 0)),
        out_shape=jax.ShapeDtypeStruct((n // HEAD_DIM, m, HEAD_DIM), BF16),
        compiler_params=_params(("arbitrary", "arbitrary")),
        name="project_heads",
    )(x, w)


def _outproj_kernel(*refs, n_x):
    x_refs, (w_ref, h_ref, g_ref, o_ref) = refs[:n_x], refs[n_x:]
    acc = None
    off = 0
    for x_ref in x_refs:
        kk = x_ref.shape[1]
        part = jnp.dot(x_ref[...], w_ref[off:off + kk, :], preferred_element_type=F32)
        acc = part if acc is None else acc + part
        off += kk
    o_ref[...] = h_ref[...] + g_ref[...] * acc


def out_project_residual(xs, w, h, mod, which_gate, row_of, tm, tn):
    m = h.shape[0]
    k = w.shape[0]
    in_specs = [pl.BlockSpec((tm, x.shape[1]), lambda j, i: (i, 0)) for x in xs]
    in_specs += [pl.BlockSpec((k, tn), lambda j, i: (0, j)),
                 pl.BlockSpec((tm, tn), lambda j, i: (i, j)),
                 pl.BlockSpec((None, None, 1, tn), lambda j, i: (row_of(i), which_gate, 0, j))]
    return pl.pallas_call(
        functools.partial(_outproj_kernel, n_x=len(xs)),
        grid=(D_MODEL // tn, m // tm),
        in_specs=in_specs,
        out_specs=pl.BlockSpec((tm, tn), lambda j, i: (i, j)),
        out_shape=jax.ShapeDtypeStruct((m, D_MODEL), F32),
        compiler_params=_params(("arbitrary", "arbitrary")),
        name="out_project_residual",
    )(*xs, w, h, mod)


def _mlp_kernel(h_ref, gn_ref, sh_ref, sc_ref, gate_ref, w1_ref, w2_ref, fg_ref, o_ref, xs_ref, hid_ref, *,
                final_norm):
    j = pl.program_id(1)

    @pl.when(j == 0)
    def _():
        _modulate_into(xs_ref, h_ref, gn_ref, sh_ref, sc_ref)
        o_ref[...] = jnp.zeros_like(o_ref)
        hid_ref[...] = jnp.zeros_like(hid_ref)

    o_ref[...] = jnp.dot(hid_ref[...], w2_ref[...], preferred_element_type=F32) + o_ref[...]
    hid = jnp.dot(xs_ref[...], w1_ref[...], preferred_element_type=F32)
    hid_ref[...] = jnp.square(jnp.maximum(hid, 0.0)).astype(BF16)

    @pl.when(j == pl.num_programs(1) - 1)
    def _():
        gate, fg = gate_ref[...], fg_ref[...]

        def rows_fn(rows):
            y = h_ref[rows, :] + gate * o_ref[rows, :]
            if final_norm:
                ms = jnp.mean(y * y, axis=-1, keepdims=True)
                y = y * lax.rsqrt(ms + EPS) * fg
            o_ref[rows, :] = y

        _row_chunks(h_ref.shape[0], rows_fn)


def mlp_residual(h, gn, mod, row_of, w1, w2, final_g, final_norm, tm, th):
    m = h.shape[0]
    nj = MLP_HIDDEN // th
    once = pl.Buffered(1)
    row = lambda i, j: (i, 0)
    return pl.pallas_call(
        functools.partial(_mlp_kernel, final_norm=final_norm),
        grid=(m // tm, nj + 1),
        in_specs=[pl.BlockSpec((tm, D_MODEL), row, pipeline_mode=once),
                  pl.BlockSpec((1, D_MODEL), lambda i, j: (0, 0)),
                  _mod_spec(3, row_of), _mod_spec(4, row_of), _mod_spec(5, row_of),
                  pl.BlockSpec((D_MODEL, th), lambda i, j: (0, jnp.minimum(j, nj - 1))),
                  pl.BlockSpec((th, D_MODEL), lambda i, j: (jnp.maximum(j - 1, 0), 0)),
                  pl.BlockSpec((1, D_MODEL), lambda i, j: (0, 0))],
        out_specs=pl.BlockSpec((tm, D_MODEL), row, pipeline_mode=once),
        out_shape=jax.ShapeDtypeStruct((m, D_MODEL), F32),
        scratch_shapes=[pltpu.VMEM((tm, D_MODEL), BF16), pltpu.VMEM((tm, th), BF16)],
        compiler_params=_params(("arbitrary", "arbitrary")),
        name="mlp_residual",
    )(h, gn.reshape(1, D_MODEL), mod, mod, mod, w1, w2, final_g.reshape(1, D_MODEL))


def _prep_kernel(x_ref, c_ref, s_ref, g_ref, o_ref):
    j = pl.program_id(1)
    x = x_ref[...].astype(F32)
    ms = jnp.mean(x * x, axis=-1, keepdims=True)
    inv = jnp.where(j < N_NORMED, lax.rsqrt(ms + EPS), 1.0)
    y = x * inv * g_ref[...]
    lane = lax.broadcasted_iota(jnp.int32, y.shape, 1)
    partner = jnp.where(lane % 64 < 32, pltpu.roll(y, LANES - 32, 1), pltpu.roll(y, 32, 1))
    o_ref[...] = (y * c_ref[...] + partner * s_ref[...]).astype(o_ref.dtype)


def prep_qk(proj, cos_t, sin_t, gains, tm):
    m = proj.shape[1]
    nt = cos_t.shape[0] // tm
    src = lambda j: jnp.where(j < N_NORMED, j, j + (P_QB - N_NORMED))
    return pl.pallas_call(
        _prep_kernel,
        grid=(m // tm, N_ROT),
        in_specs=[pl.BlockSpec((None, tm, HEAD_DIM), lambda i, j: (src(j), i, 0)),
                  pl.BlockSpec((tm, HEAD_DIM), lambda i, j: (i % nt, 0)),
                  pl.BlockSpec((tm, HEAD_DIM), lambda i, j: (i % nt, 0)),
                  pl.BlockSpec((None, 1, HEAD_DIM), lambda i, j: (j, 0, 0))],
        out_specs=pl.BlockSpec((None, tm, HEAD_DIM), lambda i, j: (j, i, 0)),
        out_shape=jax.ShapeDtypeStruct((N_ROT, m, HEAD_DIM), BF16),
        compiler_params=_params(("arbitrary", "arbitrary")),
        name="prep_qk",
    )(proj, cos_t, sin_t, gains)


def _nt_dot(a, b):
    return lax.dot_general(a, b, (((1,), (1,)), ((), ())), preferred_element_type=F32)


def _rows(c, size):
    return pl.ds(c * size, size) if isinstance(c, int) else pl.ds(pl.multiple_of(c * size, size), size)


def _transpose_to(dst_ref, src):
    dst_ref[...] = src.astype(F32).T.astype(dst_ref.dtype)


def _pipeline3(n, stage1, stage2, stage3, last=None, unroll=False):
    assert n % 2 == 0 and n >= 2
    stage1(0, 0)
    stage1(1, 1)
    stage2(0, 0)

    def pair(t, carry):
        i = 2 * t + 1
        stage3(i - 1, 0)
        stage1(i + 1, 0)
        stage2(i, 1)
        stage3(i, 1)
        stage1(i + 2, 1)
        stage2(i + 1, 0)
        return carry
    if unroll:
        for t in range((n - 2) // 2):
            pair(t, 0)
    else:
        lax.fori_loop(0, (n - 2) // 2, pair, 0)
    stage3(n - 2, 0)
    if last is not None:
        last[0](0)
    stage2(n - 1, 1)
    stage3(n - 1, 1)
    if last is not None:
        last[1](0)
        last[2](0)


def _scores(q, k, s_ref):
    s_ref[...] = _nt_dot(k, q)


def _probs(s_ref, m_ref, l_ref, p_ref, alpha_ref):
    s = s_ref[...]
    m_prev = m_ref[...]
    m_new = jnp.maximum(m_prev, jnp.max(s, axis=0, keepdims=True))
    alpha = jnp.exp2(m_prev - m_new)
    p = jnp.exp2(s - m_new)
    l_ref[...] = alpha * l_ref[...] + jnp.sum(p, axis=0, keepdims=True)
    m_ref[...] = m_new
    p_ref[...] = p.astype(BF16)
    alpha_ref[...] = alpha


def _probs_to_acc(vt, p_ref, alpha_ref, acc_ref):
    acc_ref[...] = alpha_ref[...] * acc_ref[...] + jnp.dot(vt, p_ref[...], preferred_element_type=F32)


def _gqa_kernel(*refs, with_lat, tk):
    if with_lat:
        (q_ref, kl_ref, vl_ref, kc_ref, vc_ref, o_ref,
         vlt_ref, vct_ref, m_ref, l_ref, acc_ref, s_ref, p_ref, alpha_ref) = refs
    else:
        q_ref, kc_ref, vc_ref, o_ref, vct_ref, m_ref, l_ref, acc_ref, s_ref, p_ref, alpha_ref = refs
    g, tq, d = q_ref.shape
    n_ctx = kc_ref.shape[0]

    @pl.when(pl.program_id(2) == 0)
    def _():
        _transpose_to(vct_ref, vc_ref[...])
        if with_lat:
            def body(c, carry):
                _transpose_to(vlt_ref.at[c], vl_ref[pl.ds(pl.multiple_of(c * tk, tk), tk), :])
                return carry
            lax.fori_loop(0, vlt_ref.shape[0], body, 0)

    q = q_ref[...].reshape(g * tq, d)
    m_ref[...] = jnp.full_like(m_ref, NEG)
    l_ref[...] = jnp.zeros_like(l_ref)
    acc_ref[...] = jnp.zeros_like(acc_ref)
    ctx_stages = (
        lambda slot: _scores(q, kc_ref[...], s_ref.at[slot, 0:n_ctx, :]),
        lambda slot: _probs(s_ref.at[slot, 0:n_ctx, :], m_ref, l_ref, p_ref.at[slot, 0:n_ctx, :], alpha_ref.at[slot]),
        lambda slot: _probs_to_acc(vct_ref[...], p_ref.at[slot, 0:n_ctx, :], alpha_ref.at[slot], acc_ref))
    if with_lat:
        rows = lambda c: _rows(c, tk)
        _pipeline3(
            vlt_ref.shape[0],
            lambda c, slot: _scores(q, kl_ref[rows(c), :], s_ref.at[slot]),
            lambda c, slot: _probs(s_ref.at[slot], m_ref, l_ref, p_ref.at[slot], alpha_ref.at[slot]),
            lambda c, slot: _probs_to_acc(vlt_ref[c], p_ref.at[slot], alpha_ref.at[slot], acc_ref),
            last=ctx_stages, unroll=True)
    else:
        for stage in ctx_stages:
            stage(0)
    o = (acc_ref[...] * (1.0 / l_ref[...])).T
    for h in range(g):
        o_ref[:, h * d:(h + 1) * d] = o[h * tq:(h + 1) * tq].astype(o_ref.dtype)


def gqa_attention(rot_q, proj_q, rot_lat, proj_lat, rot_ctx, proj_ctx, tq, tk):
    mq = rot_q.shape[1]
    nq = mq // BATCH // tq
    with_lat = rot_lat is not None
    q_spec = pl.BlockSpec((A_GROUP, tq, HEAD_DIM), lambda b, g, i: (g, b * nq + i, 0))
    kv = lambda base, n: pl.BlockSpec((None, n, HEAD_DIM), lambda b, g, i: (base + g, b, 0))
    in_specs, args = [q_spec], [rot_q]
    if with_lat:
        in_specs += [kv(R_KA, SEQ), kv(P_VA, SEQ)]
        args += [rot_lat, proj_lat]
    in_specs += [kv(R_KA, CTX_LEN), kv(P_VA, CTX_LEN)]
    args += [rot_ctx, proj_ctx]
    rows = A_GROUP * tq
    scratch = [pltpu.VMEM((SEQ // tk, HEAD_DIM, tk), BF16)] if with_lat else []
    n_p = max(tk, CTX_LEN) if with_lat else CTX_LEN
    scratch += [pltpu.VMEM((HEAD_DIM, CTX_LEN), BF16),
                pltpu.VMEM((1, rows), F32), pltpu.VMEM((1, rows), F32), pltpu.VMEM((HEAD_DIM, rows), F32),
                pltpu.VMEM((2, n_p, rows), F32), pltpu.VMEM((2, n_p, rows), BF16), pltpu.VMEM((2, 1, rows), F32)]
    return pl.pallas_call(
        functools.partial(_gqa_kernel, with_lat=with_lat, tk=tk),
        grid=(BATCH, A_KV_HEADS, nq),
        in_specs=in_specs,
        out_specs=pl.BlockSpec((tq, A_GROUP * HEAD_DIM), lambda b, g, i: (b * nq + i, g)),
        out_shape=jax.ShapeDtypeStruct((mq, A_HEADS * HEAD_DIM), BF16),
        scratch_shapes=scratch,
        compiler_params=_params(("arbitrary", "arbitrary", "arbitrary")),
        name="gqa_attention",
    )(*args)


def _diff_kernel(*refs, with_lat, tk, lambda_init):
    if with_lat:
        (q_ref, kl_ref, vl_ref, kc_ref, vc_ref, lq1_ref, lk1_ref, lq2_ref, lk2_ref, sg_ref,
         o_ref, vlt_ref, vct_ref, m_ref, l_ref, acc_ref, s_ref, p_ref, alpha_ref) = refs
    else:
        (q_ref, kc_ref, vc_ref, lq1_ref, lk1_ref, lq2_ref, lk2_ref, sg_ref,
         o_ref, vct_ref, m_ref, l_ref, acc_ref, s_ref, p_ref, alpha_ref) = refs
    d = q_ref.shape[2]

    @pl.when(pl.program_id(2) == 0)
    def _():
        for half in range(2):
            _transpose_to(vct_ref.at[half * d:(half + 1) * d, :], vc_ref[half])
        if with_lat:
            def body(c, carry):
                rows = pl.ds(pl.multiple_of(c * tk, tk), tk)
                for half in range(2):
                    _transpose_to(vlt_ref.at[c, half * d:(half + 1) * d, :], vl_ref[half, rows, :])
                return carry
            lax.fori_loop(0, vlt_ref.shape[0], body, 0)

    m_ref[...] = jnp.full_like(m_ref, NEG)
    l_ref[...] = jnp.zeros_like(l_ref)
    acc_ref[...] = jnp.zeros_like(acc_ref)

    def scores(k_of, s_of):
        for i in range(2):
            _scores(q_ref[i], k_of(i), s_of(i))

    def probs(s_of, p_of, slot):
        for i in range(2):
            _probs(s_of(i), m_ref.at[i], l_ref.at[i], p_of(i), alpha_ref.at[slot, i])

    def to_acc(vt, p_of, slot):
        for i in range(2):
            _probs_to_acc(vt, p_of(i), alpha_ref.at[slot, i], acc_ref.at[i])

    n_ctx = kc_ref.shape[1]
    s_ctx = lambda slot: lambda i: s_ref.at[slot, i, 0:n_ctx, :]
    p_ctx = lambda slot: lambda i: p_ref.at[slot, i, 0:n_ctx, :]
    ctx_stages = (
        lambda slot: scores(lambda i: kc_ref[i], s_ctx(slot)),
        lambda slot: probs(s_ctx(slot), p_ctx(slot), slot),
        lambda slot: to_acc(vct_ref[...], p_ctx(slot), slot))
    if with_lat:
        rows = lambda c: _rows(c, tk)
        _pipeline3(
            vlt_ref.shape[0],
            lambda c, slot: scores(lambda i: kl_ref[i, rows(c), :], lambda i: s_ref.at[slot, i]),
            lambda c, slot: probs(lambda i: s_ref.at[slot, i], lambda i: p_ref.at[slot, i], slot),
            lambda c, slot: to_acc(vlt_ref[c], lambda i: p_ref.at[slot, i], slot),
            last=ctx_stages, unroll=True)
    else:
        for stage in ctx_stages:
            stage(0)

    lam = (jnp.exp(jnp.sum(lq1_ref[...] * lk1_ref[...], axis=1, keepdims=True))
           - jnp.exp(jnp.sum(lq2_ref[...] * lk2_ref[...], axis=1, keepdims=True)) + lambda_init)
    o = (acc_ref[0] * (1.0 / l_ref[0]) - lam * (acc_ref[1] * (1.0 / l_ref[1]))).T
    ms = jnp.mean(o * o, axis=-1, keepdims=True)
    o_ref[...] = ((o * lax.rsqrt(ms + EPS) * sg_ref[...]) * (1.0 - lambda_init)).astype(o_ref.dtype)


def diff_attention(rot_q, rot_lat, proj_lat, rot_ctx, proj_ctx, lam_params, subln_g, lambda_init, tq, tk):
    mq = rot_q.shape[1]
    nq = mq // BATCH // tq
    with_lat = rot_lat is not None
    pair = lambda base, n: pl.BlockSpec((2, n, HEAD_DIM), lambda b, h, i: (base // 2 + h, b, 0))
    in_specs = [pl.BlockSpec((2, tq, HEAD_DIM), lambda b, h, i: (R_QB // 2 + h, b * nq + i, 0))]
    args = [rot_q]
    if with_lat:
        in_specs += [pair(R_KB, SEQ), pair(P_VB, SEQ)]
        args += [rot_lat, proj_lat]
    in_specs += [pair(R_KB, CTX_LEN), pair(P_VB, CTX_LEN)]
    args += [rot_ctx, proj_ctx]
    vec = lambda n: pl.BlockSpec((1, n), lambda b, h, i: (0, 0))
    in_specs += [vec(HEAD_DIM)] * 4 + [vec(2 * HEAD_DIM)]
    args += [p.reshape(1, HEAD_DIM) for p in lam_params] + [subln_g.reshape(1, 2 * HEAD_DIM)]
    scratch = [pltpu.VMEM((SEQ // tk, 2 * HEAD_DIM, tk), BF16)] if with_lat else []
    n_p = max(tk, CTX_LEN) if with_lat else CTX_LEN
    scratch += [pltpu.VMEM((2 * HEAD_DIM, CTX_LEN), BF16),
                pltpu.VMEM((2, 1, tq), F32), pltpu.VMEM((2, 1, tq), F32), pltpu.VMEM((2, 2 * HEAD_DIM, tq), F32),
                pltpu.VMEM((2, 2, n_p, tq), F32), pltpu.VMEM((2, 2, n_p, tq), BF16),
                pltpu.VMEM((2, 2, 1, tq), F32)]
    return pl.pallas_call(
        functools.partial(_diff_kernel, with_lat=with_lat, tk=tk, lambda_init=lambda_init),
        grid=(BATCH, B_HEADS, nq),
        in_specs=in_specs,
        out_specs=pl.BlockSpec((tq, 2 * HEAD_DIM), lambda b, h, i: (b * nq + i, h)),
        out_shape=jax.ShapeDtypeStruct((mq, B_HEADS * 2 * HEAD_DIM), BF16),
        scratch_shapes=scratch,
        compiler_params=_params(("arbitrary", "arbitrary", "arbitrary")),
        name="diff_attention",
    )(*args)


NA_QROWS = 4
NA_CHUNK = NA_QROWS * GRID_W
NA_BAND_CHUNKS = 3
N_DR = 2 * NA_ROWS - 1
NA_ONLY_A, NA_ONLY_B, NA_NEITHER = N_DR - 1, 2 * N_DR - 1, 3 * N_DR - 1


def _na_tile_index(key_row, row_a):
    da = key_row - row_a + (NA_ROWS - 1)
    if isinstance(key_row, int) and isinstance(row_a, int):
        first = lambda r: min(max(r - NA_ROWS // 2, 0), ROWS - NA_ROWS)
        va, vb = (first(r) <= key_row < first(r) + NA_ROWS for r in (row_a, row_a + 1))
        return da - 1 if (va and vb) else NA_ONLY_A + da if va else NA_ONLY_B - 1 + da if vb else NA_NEITHER
    first = lambda r: jnp.clip(r - NA_ROWS // 2, 0, ROWS - NA_ROWS)
    sees = lambda r: jnp.logical_and(key_row >= first(r), key_row < first(r) + NA_ROWS)
    va, vb = sees(row_a), sees(row_a + 1)
    return jnp.where(jnp.logical_and(va, vb), da - 1,
                     jnp.where(va, NA_ONLY_A + da, jnp.where(vb, NA_ONLY_B - 1 + da, NA_NEITHER)))


def _na_kernel(q_ref, k_ref, v_ref, kc_ref, vc_ref, tab_ref, o_ref, vt_ref, vct_ref, s_ref, p_ref, linv_ref):
    n_chunks = vt_ref.shape[0]
    chunk = lambda c: _rows(c, NA_CHUNK)

    def band_start(blk):
        if isinstance(blk, int):
            return min(max(blk - 1, 0), n_chunks - NA_BAND_CHUNKS)
        return jnp.clip(blk - 1, 0, n_chunks - NA_BAND_CHUNKS)

    _transpose_to(vct_ref, vc_ref[...])

    def transpose_chunk(c, carry):
        _transpose_to(vt_ref.at[c], v_ref[chunk(c), :])
        return carry
    lax.fori_loop(0, n_chunks, transpose_chunk, 0)

    def scores(blk, slot):
        row0 = blk * NA_QROWS
        cb = band_start(blk)
        q = q_ref[chunk(blk), :]
        for c in range(NA_BAND_CHUNKS):
            tiles = []
            for i in range(NA_QROWS):
                key_row = (cb + c) * NA_QROWS + i
                tiles.append(jnp.concatenate(
                    [tab_ref[_na_tile_index(key_row, row0 + 2 * j)] for j in range(NA_QROWS // 2)], axis=1))
            s_ref[slot, c] = _nt_dot(k_ref[chunk(cb + c), :], q) * SCALE_LOG2 + jnp.concatenate(tiles, axis=0)
        s_ref[slot, NA_BAND_CHUNKS] = _nt_dot(kc_ref[...], q) * SCALE_LOG2

    def probs(blk, slot):
        n = NA_BAND_CHUNKS + 1
        m = functools.reduce(jnp.maximum, [jnp.max(s_ref[slot, c], axis=0, keepdims=True) for c in range(n)])
        l = None
        for c in range(n):
            p = jnp.exp2(s_ref[slot, c] - m)
            ps = jnp.sum(p, axis=0, keepdims=True)
            l = ps if l is None else l + ps
            p_ref[slot, c] = p.astype(BF16)
        linv_ref[slot] = 1.0 / l

    def output(blk, slot):
        cb = band_start(blk)
        o_t = jnp.dot(vct_ref[...], p_ref[slot, NA_BAND_CHUNKS], preferred_element_type=F32)
        for c in range(NA_BAND_CHUNKS):
            o_t = o_t + jnp.dot(vt_ref[cb + c], p_ref[slot, c], preferred_element_type=F32)
        o_ref[chunk(blk), :] = (o_t * linv_ref[slot]).T.astype(o_ref.dtype)

    _pipeline3(n_chunks, scores, probs, output, unroll=True)


def neighbourhood_attention(proj_lat, proj_ctx, bias_table):
    assert CTX_LEN == NA_CHUNK
    lat = lambda base: pl.BlockSpec((None, SEQ, HEAD_DIM), lambda b, h: (base + h, b, 0))
    ctx = lambda base: pl.BlockSpec((None, CTX_LEN, HEAD_DIM), lambda b, h: (base + h, b, 0))
    return pl.pallas_call(
        _na_kernel,
        grid=(BATCH, C_HEADS),
        in_specs=[lat(0), lat(C_HEADS), lat(2 * C_HEADS), ctx(0), ctx(C_HEADS),
                  pl.BlockSpec((None, 3 * N_DR, GRID_W, 2 * GRID_W), lambda b, h: (h, 0, 0, 0))],
        out_specs=pl.BlockSpec((SEQ, HEAD_DIM), lambda b, h: (b, h)),
        out_shape=jax.ShapeDtypeStruct((BATCH * SEQ, C_HEADS * HEAD_DIM), BF16),
        scratch_shapes=[pltpu.VMEM((SEQ // NA_CHUNK, HEAD_DIM, NA_CHUNK), BF16),
                        pltpu.VMEM((HEAD_DIM, CTX_LEN), BF16),
                        pltpu.VMEM((2, NA_BAND_CHUNKS + 1, NA_CHUNK, NA_CHUNK), F32),
                        pltpu.VMEM((2, NA_BAND_CHUNKS + 1, NA_CHUNK, NA_CHUNK), BF16),
                        pltpu.VMEM((2, 1, NA_CHUNK), F32)],
        compiler_params=_params(("arbitrary", "arbitrary")),
        name="neighbourhood_attention",
    )(proj_lat, proj_lat, proj_lat, proj_ctx, proj_ctx, bias_table)


def _rope_tables():
    t = jnp.arange(SEQ, dtype=jnp.int32)
    row = (t // GRID_W).astype(F32)
    col = (t % GRID_W).astype(F32)
    n_freq = HEAD_DIM // 4
    freqs = ROPE_THETA ** (-jnp.arange(n_freq, dtype=F32) / n_freq)
    ar, ac = row[:, None] * freqs, col[:, None] * freqs
    cos_t = jnp.concatenate([jnp.cos(ar), jnp.cos(ar), jnp.cos(ac), jnp.cos(ac)], axis=1)
    sin_t = jnp.concatenate([-jnp.sin(ar), jnp.sin(ar), -jnp.sin(ac), jnp.sin(ac)], axis=1)
    return cos_t, sin_t


def _na_bias_table(rel_bias):
    col = jnp.arange(GRID_W, dtype=jnp.int32)
    c0 = jnp.clip(col - NA_COLS // 2, 0, GRID_W - NA_COLS)
    key, qry = col[:, None], col[None, :]
    dc = key - qry + (NA_COLS - 1)
    valid = (key >= c0[None, :]) & (key < c0[None, :] + NA_COLS)
    t = rel_bias[:, :, jnp.clip(dc, 0, 2 * NA_COLS - 2)].astype(F32) * LOG2E
    t = jnp.where(valid[None, None], t, NEG)
    off = jnp.full_like(t, NEG)
    both = jnp.concatenate([t[:, 1:], t[:, :-1]], axis=-1)
    only_a = jnp.concatenate([t, off], axis=-1)
    only_b = jnp.concatenate([off, t], axis=-1)
    neither = jnp.concatenate([off[:, :1], off[:, :1]], axis=-1)
    return jnp.concatenate([both, only_a, only_b, neither], axis=1)


def kernel(x, c, ctx, c_ctx, ada_w, ada_b, norm1_g, norm2_g, w_in_even, w_out_even, a_q_norm, a_k_norm,
           b_lambda_q1, b_lambda_k1, b_lambda_q2, b_lambda_k2, b_subln_g, w_in_odd, w_out_odd, na_rel_bias,
           mlp_w1, mlp_w2, final_g):
    TM = 1024
    TC = BATCH * CTX_LEN
    TN = 512
    lat_row = lambda m: m // (SEQ // TM)
    lat_row_n = lambda m: m // (SEQ // TN)
    ctx_row = lambda m: 2

    craw = jnp.concatenate([c, c_ctx[None], jnp.zeros((5, D_MODEL), F32)], axis=0)
    mod = ada_modulation(craw, ada_w, ada_b).reshape(DEPTH, 8, 6, 1, D_MODEL)

    h = x.reshape(BATCH * SEQ, D_MODEL)
    hc = ctx.reshape(TC, D_MODEL)
    cos_t, sin_t = _rope_tables()
    one_t, zero_t = jnp.ones((TC, HEAD_DIM), F32), jnp.zeros((TC, HEAD_DIM), F32)

    mod0 = mod[0]
    w_in = w_in_even[0].astype(BF16)
    w_out = w_out_even[0].astype(BF16)
    ones = jnp.ones((HEAD_DIM,), F32)
    gains = jnp.stack([a_q_norm[0] * SCALE_LOG2] * A_HEADS + [a_k_norm[0]] * A_KV_HEADS
                      + [ones * SCALE_LOG2] * (2 * B_HEADS) + [ones] * (2 * B_HEADS)).reshape(N_ROT, 1, HEAD_DIM)
    lam_params = (b_lambda_q1[0], b_lambda_k1[0], b_lambda_q2[0], b_lambda_k2[0])
    lambda_init = 0.8 - 0.6 * math.exp(-0.3 * 0)

    u = norm_modulate(h, norm1_g[0], mod0, 0, 1, lat_row_n, TN)
    uc = norm_modulate(hc, norm1_g[0], mod0, 0, 1, ctx_row, TC)
    proj = project_heads(u, w_in, TM, 1024)
    proj_c = project_heads(uc, w_in, TC, 1024)
    rot = prep_qk(proj, cos_t, sin_t, gains, 2048)
    rot_c = prep_qk(proj_c, one_t, zero_t, gains, TC)

    a_lat = gqa_attention(rot, proj, rot, proj, rot_c, proj_c, 128, 512)
    b_lat = diff_attention(rot, rot, proj, rot_c, proj_c, lam_params, b_subln_g[0], lambda_init, 512, 512)
    a_ctx = gqa_attention(rot_c, proj_c, None, None, rot_c, proj_c, 128, 512)
    b_ctx = diff_attention(rot_c, None, None, rot_c, proj_c, lam_params, b_subln_g[0], lambda_init, 256, 512)

    h = out_project_residual([a_lat, b_lat], w_out, h, mod0, 2, lat_row, TM, 1024)
    hc = out_project_residual([a_ctx, b_ctx], w_out, hc, mod0, 2, ctx_row, TC, 1024)
    w1 = mlp_w1[0].astype(BF16)
    w2 = mlp_w2[0].astype(BF16)
    h = mlp_residual(h, norm2_g[0], mod0, lat_row, w1, w2, final_g, False, TM, 512)
    hc = mlp_residual(hc, norm2_g[0], mod0, ctx_row, w1, w2, final_g, False, TC, 256)

    mod1 = mod[1]
    w_in = w_in_odd[0].astype(BF16)
    u = norm_modulate(h, norm1_g[1], mod1, 0, 1, lat_row_n, TN)
    uc = norm_modulate(hc, norm1_g[1], mod1, 0, 1, ctx_row, TC)
    proj = project_heads(u, w_in, TM, 1024)
    proj_c = project_heads(uc, w_in[:, C_HEADS * HEAD_DIM:], TC, 1024)
    attn = neighbourhood_attention(proj, proj_c, _na_bias_table(na_rel_bias[0]))
    h = out_project_residual([attn], w_out_odd[0].astype(BF16), h, mod1, 2, lat_row, TM, 1024)
    out = mlp_residual(h, norm2_g[1], mod1, lat_row, mlp_w1[1].astype(BF16), mlp_w2[1].astype(BF16),
                       final_g, True, TM, 512)
    return out.reshape(BATCH, SEQ, D_MODEL)
```

```python
import functools
import math

import jax
import jax.numpy as jnp
from jax import lax
from jax.experimental import pallas as pl
from jax.experimental.pallas import tpu as pltpu

D_MODEL = 4096
BATCH = 2
SEQ = 8192
DEPTH = 2
GRID_W = 64
ROWS = SEQ // GRID_W
CTX_LEN = 256
HEAD_DIM = 128
ROPE_THETA = 10000.0
EPS = 1e-6
A_HEADS = 16
A_KV_HEADS = 4
A_GROUP = A_HEADS // A_KV_HEADS
B_HEADS = 8
C_HEADS = 32
NA_ROWS = 8
NA_COLS = 16
MLP_HIDDEN = 4 * D_MODEL
EVEN_IN = 9216
SCALE = HEAD_DIM ** -0.5
LOG2E = 1.4426950408889634
SCALE_LOG2 = SCALE * LOG2E

P_QA, P_KA, P_VA, P_QB, P_KB, P_VB = 0, 16, 20, 24, 40, 56
N_PROJ_EVEN = EVEN_IN // HEAD_DIM
R_QA, R_KA, R_QB, R_KB = 0, 16, 20, 36
N_ROT = 52
N_NORMED = 20

LANES = 128
VMEM_LIMIT = 62 * 1024 * 1024
NEG = -1e30

F32 = jnp.float32
BF16 = jnp.bfloat16


def _params(sem, vmem=VMEM_LIMIT):
    return pltpu.CompilerParams(dimension_semantics=sem, vmem_limit_bytes=vmem)


def _ada_kernel(c_ref, w_ref, b_ref, o_ref):
    x = c_ref[...]
    cond = x / (1.0 + jnp.exp(-x))
    o_ref[...] = jnp.dot(cond.astype(BF16), w_ref[...].astype(BF16),
                         preferred_element_type=F32) + b_ref[...]


def ada_modulation(craw, ada_w, ada_b):
    tn = 512
    n6 = 6 * D_MODEL
    return pl.pallas_call(
        _ada_kernel,
        grid=(DEPTH, n6 // tn),
        in_specs=[pl.BlockSpec((8, D_MODEL), lambda i, n: (0, 0)),
                  pl.BlockSpec((None, D_MODEL, tn), lambda i, n: (i, 0, n)),
                  pl.BlockSpec((None, 1, tn), lambda i, n: (i, 0, n))],
        out_specs=pl.BlockSpec((None, 8, tn), lambda i, n: (i, 0, n)),
        out_shape=jax.ShapeDtypeStruct((DEPTH, 8, n6), F32),
        compiler_params=_params(("arbitrary", "arbitrary")),
        name="ada_modulation",
    )(craw, ada_w, ada_b.reshape(DEPTH, 1, n6))


ROW_CHUNK = 16


def _row_chunks(n_rows, fn):
    def body(i, carry):
        fn(pl.ds(pl.multiple_of(i * ROW_CHUNK, ROW_CHUNK), ROW_CHUNK))
        return carry
    lax.fori_loop(0, n_rows // ROW_CHUNK, body, 0)


def _modulate_into(dst_ref, h_ref, g_ref, sh_ref, sc_ref):
    g, up, sh = g_ref[...], 1.0 + sc_ref[...], sh_ref[...]

    def rows_fn(rows):
        x = h_ref[rows, :]
        ms = jnp.mean(x * x, axis=-1, keepdims=True)
        dst_ref[rows, :] = ((x * lax.rsqrt(ms + EPS) * g) * up + sh).astype(dst_ref.dtype)

    _row_chunks(h_ref.shape[0], rows_fn)


def _normmod_kernel(h_ref, g_ref, sh_ref, sc_ref, o_ref):
    _modulate_into(o_ref, h_ref, g_ref, sh_ref, sc_ref)


def _mod_spec(which, row_of):
    return pl.BlockSpec((None, None, 1, D_MODEL), lambda m, *_: (row_of(m), which, 0, 0))


def norm_modulate(h, g, mod, which_shift, which_scale, row_of, tm):
    m = h.shape[0]
    return pl.pallas_call(
        _normmod_kernel,
        grid=(m // tm,),
        in_specs=[pl.BlockSpec((tm, D_MODEL), lambda i: (i, 0)),
                  pl.BlockSpec((1, D_MODEL), lambda i: (0, 0)),
                  _mod_spec(which_shift, row_of), _mod_spec(which_scale, row_of)],
        out_specs=pl.BlockSpec((tm, D_MODEL), lambda i: (i, 0)),
        out_shape=jax.ShapeDtypeStruct((m, D_MODEL), BF16),
        compiler_params=_params(("arbitrary",)),
        name="norm_modulate",
    )(h, g.reshape(1, D_MODEL), mod, mod)


def _proj_kernel(x_ref, w_ref, o_ref):
    acc = jnp.dot(x_ref[...], w_ref[...], preferred_element_type=F32)
    for j in range(o_ref.shape[0]):
        o_ref[j] = acc[:, j * HEAD_DIM:(j + 1) * HEAD_DIM].astype(o_ref.dtype)


def project_heads(x, w, tm, tn):
    m, k = x.shape
    n = w.shape[1]
    return pl.pallas_call(
        _proj_kernel,
        grid=(n // tn, m // tm),
        in_specs=[pl.BlockSpec((tm, k), lambda j, i: (i, 0)),
                  pl.BlockSpec((k, tn), lambda j, i: (0, j))],
        out_specs=pl.BlockSpec((tn // HEAD_DIM, tm, HEAD_DIM), lambda j, i: (j, i, 0)),
        out_shape=jax.ShapeDtypeStruct((n // HEAD_DIM, m, HEAD_DIM), BF16),
        compiler_params=_params(("arbitrary", "arbitrary")),
        name="project_heads",
    )(x, w)


def _outproj_kernel(*refs, n_x):
    x_refs, (w_ref, h_ref, g_ref, o_ref) = refs[:n_x], refs[n_x:]
    acc = None
    off = 0
    for x_ref in x_refs:
        kk = x_ref.shape[1]
        part = jnp.dot(x_ref[...], w_ref[off:off + kk, :], preferred_element_type=F32)
        acc = part if acc is None else acc + part
        off += kk
    o_ref[...] = h_ref[...] + g_ref[...] * acc


def out_project_residual(xs, w, h, mod, which_gate, row_of, tm, tn):
    m = h.shape[0]
    k = w.shape[0]
    in_specs = [pl.BlockSpec((tm, x.shape[1]), lambda j, i: (i, 0)) for x in xs]
    in_specs += [pl.BlockSpec((k, tn), lambda j, i: (0, j)),
                 pl.BlockSpec((tm, tn), lambda j, i: (i, j)),
                 pl.BlockSpec((None, None, 1, tn), lambda j, i: (row_of(i), which_gate, 0, j))]
    return pl.pallas_call(
        functools.partial(_outproj_kernel, n_x=len(xs)),
        grid=(D_MODEL // tn, m // tm),
        in_specs=in_specs,
        out_specs=pl.BlockSpec((tm, tn), lambda j, i: (i, j)),
        out_shape=jax.ShapeDtypeStruct((m, D_MODEL), F32),
        compiler_params=_params(("arbitrary", "arbitrary")),
        name="out_project_residual",
    )(*xs, w, h, mod)


def _mlp_kernel(h_ref, gn_ref, sh_ref, sc_ref, gate_ref, w1_ref, w2_ref, fg_ref, o_ref, xs_ref, hid_ref, *,
                final_norm):
    j = pl.program_id(1)

    @pl.when(j == 0)
    def _():
        _modulate_into(xs_ref, h_ref, gn_ref, sh_ref, sc_ref)
        o_ref[...] = jnp.zeros_like(o_ref)
        hid_ref[...] = jnp.zeros_like(hid_ref)

    o_ref[...] = jnp.dot(hid_ref[...], w2_ref[...], preferred_element_type=F32) + o_ref[...]
    hid = jnp.dot(xs_ref[...], w1_ref[...], preferred_element_type=F32)
    hid_ref[...] = jnp.square(jnp.maximum(hid, 0.0)).astype(BF16)

    @pl.when(j == pl.num_programs(1) - 1)
    def _():
        gate, fg = gate_ref[...], fg_ref[...]

        def rows_fn(rows):
            y = h_ref[rows, :] + gate * o_ref[rows, :]
            if final_norm:
                ms = jnp.mean(y * y, axis=-1, keepdims=True)
                y = y * lax.rsqrt(ms + EPS) * fg
            o_ref[rows, :] = y

        _row_chunks(h_ref.shape[0], rows_fn)


def mlp_residual(h, gn, mod, row_of, w1, w2, final_g, final_norm, tm, th):
    m = h.shape[0]
    nj = MLP_HIDDEN // th
    once = pl.Buffered(1)
    row = lambda i, j: (i, 0)
    return pl.pallas_call(
        functools.partial(_mlp_kernel, final_norm=final_norm),
        grid=(m // tm, nj + 1),
        in_specs=[pl.BlockSpec((tm, D_MODEL), row, pipeline_mode=once),
                  pl.BlockSpec((1, D_MODEL), lambda i, j: (0, 0)),
                  _mod_spec(3, row_of), _mod_spec(4, row_of), _mod_spec(5, row_of),
                  pl.BlockSpec((D_MODEL, th), lambda i, j: (0, jnp.minimum(j, nj - 1))),
                  pl.BlockSpec((th, D_MODEL), lambda i, j: (jnp.maximum(j - 1, 0), 0)),
                  pl.BlockSpec((1, D_MODEL), lambda i, j: (0, 0))],
        out_specs=pl.BlockSpec((tm, D_MODEL), row, pipeline_mode=once),
        out_shape=jax.ShapeDtypeStruct((m, D_MODEL), F32),
        scratch_shapes=[pltpu.VMEM((tm, D_MODEL), BF16), pltpu.VMEM((tm, th), BF16)],
        compiler_params=_params(("arbitrary", "arbitrary")),
        name="mlp_residual",
    )(h, gn.reshape(1, D_MODEL), mod, mod, mod, w1, w2, final_g.reshape(1, D_MODEL))


def _prep_kernel(x_ref, c_ref, s_ref, g_ref, o_ref):
    j = pl.program_id(1)
    x = x_ref[...].astype(F32)
    ms = jnp.mean(x * x, axis=-1, keepdims=True)
    inv = jnp.where(j < N_NORMED, lax.rsqrt(ms + EPS), 1.0)
    y = x * inv * g_ref[...]
    lane = lax.broadcasted_iota(jnp.int32, y.shape, 1)
    partner = jnp.where(lane % 64 < 32, pltpu.roll(y, LANES - 32, 1), pltpu.roll(y, 32, 1))
    o_ref[...] = (y * c_ref[...] + partner * s_ref[...]).astype(o_ref.dtype)


def prep_qk(proj, cos_t, sin_t, gains, tm):
    m = proj.shape[1]
    nt = cos_t.shape[0] // tm
    src = lambda j: jnp.where(j < N_NORMED, j, j + (P_QB - N_NORMED))
    return pl.pallas_call(
        _prep_kernel,
        grid=(m // tm, N_ROT),
        in_specs=[pl.BlockSpec((None, tm, HEAD_DIM), lambda i, j: (src(j), i, 0)),
                  pl.BlockSpec((tm, HEAD_DIM), lambda i, j: (i % nt, 0)),
                  pl.BlockSpec((tm, HEAD_DIM), lambda i, j: (i % nt, 0)),
                  pl.BlockSpec((None, 1, HEAD_DIM), lambda i, j: (j, 0, 0))],
        out_specs=pl.BlockSpec((None, tm, HEAD_DIM), lambda i, j: (j, i, 0)),
        out_shape=jax.ShapeDtypeStruct((N_ROT, m, HEAD_DIM), BF16),
        compiler_params=_params(("arbitrary", "arbitrary")),
        name="prep_qk",
    )(proj, cos_t, sin_t, gains)


def _nt_dot(a, b):
    return lax.dot_general(a, b, (((1,), (1,)), ((), ())), preferred_element_type=F32)


def _rows(c, size):
    return pl.ds(c * size, size) if isinstance(c, int) else pl.ds(pl.multiple_of(c * size, size), size)


def _transpose_to(dst_ref, src):
    dst_ref[...] = src.astype(F32).T.astype(dst_ref.dtype)


def _pipeline3(n, stage1, stage2, stage3):
    if n == 1:
        stage3(0, stage2(0, stage1(0)))
        return
    x = {0: stage1(0), 1: stage1(1)}
    y = {0: stage2(0, x.pop(0))}
    for i in range(1, n):
        stage3(i - 1, y.pop(i - 1))
        if i + 1 < n:
            x[i + 1] = stage1(i + 1)
        y[i] = stage2(i, x.pop(i))
    stage3(n - 1, y.pop(n - 1))


class _OnlineSoftmax:
    def __init__(self, n_queries, dv):
        self.m = jnp.full((1, n_queries), NEG, F32)
        self.l = jnp.zeros((1, n_queries), F32)
        self.acc = jnp.zeros((dv, n_queries), F32)

    def probs(self, s):
        m_new = jnp.maximum(self.m, jnp.max(s, axis=0, keepdims=True))
        alpha = jnp.exp2(self.m - m_new)
        p = jnp.exp2(s - m_new)
        self.l = alpha * self.l + jnp.sum(p, axis=0, keepdims=True)
        self.m = m_new
        return p.astype(BF16), alpha

    def accumulate(self, vt, p_alpha):
        p, alpha = p_alpha
        self.acc = alpha * self.acc + jnp.dot(vt, p, preferred_element_type=F32)

    def result(self):
        return self.acc * (1.0 / self.l)


def _gqa_kernel(*refs, with_lat, tk):
    if with_lat:
        q_ref, kl_ref, vl_ref, kc_ref, vc_ref, o_ref, vlt_ref, vct_ref = refs
    else:
        q_ref, kc_ref, vc_ref, o_ref, vct_ref = refs
    g, tq, d = q_ref.shape
    n_lat = vlt_ref.shape[0] if with_lat else 0

    @pl.when(pl.program_id(2) == 0)
    def _():
        _transpose_to(vct_ref, vc_ref[...])
        if with_lat:
            def body(c, carry):
                _transpose_to(vlt_ref.at[c], vl_ref[_rows(c, tk), :])
                return carry
            lax.fori_loop(0, n_lat, body, 0)

    q = q_ref[...].reshape(g * tq, d)
    sm = _OnlineSoftmax(g * tq, d)
    keys = lambda c: kc_ref[...] if c == n_lat else kl_ref[_rows(c, tk), :]
    vals = lambda c: vct_ref[...] if c == n_lat else vlt_ref[c]
    _pipeline3(n_lat + 1,
               lambda c: _nt_dot(keys(c), q),
               lambda c, s: sm.probs(s),
               lambda c, pa: sm.accumulate(vals(c), pa))
    o = sm.result().T
    for h in range(g):
        o_ref[:, h * d:(h + 1) * d] = o[h * tq:(h + 1) * tq].astype(o_ref.dtype)


def gqa_attention(rot_q, proj_q, rot_lat, proj_lat, rot_ctx, proj_ctx, tq, tk):
    mq = rot_q.shape[1]
    nq = mq // BATCH // tq
    with_lat = rot_lat is not None
    q_spec = pl.BlockSpec((A_GROUP, tq, HEAD_DIM), lambda b, g, i: (g, b * nq + i, 0))
    kv = lambda base, n: pl.BlockSpec((None, n, HEAD_DIM), lambda b, g, i: (base + g, b, 0))
    in_specs, args = [q_spec], [rot_q]
    if with_lat:
        in_specs += [kv(R_KA, SEQ), kv(P_VA, SEQ)]
        args += [rot_lat, proj_lat]
    in_specs += [kv(R_KA, CTX_LEN), kv(P_VA, CTX_LEN)]
    args += [rot_ctx, proj_ctx]
    scratch = [pltpu.VMEM((SEQ // tk, HEAD_DIM, tk), BF16)] if with_lat else []
    scratch += [pltpu.VMEM((HEAD_DIM, CTX_LEN), BF16)]
    return pl.pallas_call(
        functools.partial(_gqa_kernel, with_lat=with_lat, tk=tk),
        grid=(BATCH, A_KV_HEADS, nq),
        in_specs=in_specs,
        out_specs=pl.BlockSpec((tq, A_GROUP * HEAD_DIM), lambda b, g, i: (b * nq + i, g)),
        out_shape=jax.ShapeDtypeStruct((mq, A_HEADS * HEAD_DIM), BF16),
        scratch_shapes=scratch,
        compiler_params=_params(("arbitrary", "arbitrary", "arbitrary")),
        name="gqa_attention",
    )(*args)


def _diff_kernel(*refs, with_lat, tk, lambda_init):
    if with_lat:
        (q_ref, kl_ref, vl_ref, kc_ref, vc_ref, lq1_ref, lk1_ref, lq2_ref, lk2_ref, sg_ref,
         o_ref, vlt_ref, vct_ref) = refs
    else:
        q_ref, kc_ref, vc_ref, lq1_ref, lk1_ref, lq2_ref, lk2_ref, sg_ref, o_ref, vct_ref = refs
    tq, d = q_ref.shape[1:]
    n_lat = vlt_ref.shape[0] if with_lat else 0

    @pl.when(pl.program_id(2) == 0)
    def _():
        for half in range(2):
            _transpose_to(vct_ref.at[half * d:(half + 1) * d, :], vc_ref[half])
        if with_lat:
            def body(c, carry):
                for half in range(2):
                    _transpose_to(vlt_ref.at[c, half * d:(half + 1) * d, :], vl_ref[half, _rows(c, tk), :])
                return carry
            lax.fori_loop(0, n_lat, body, 0)

    maps = (0, 1)
    sm = [_OnlineSoftmax(tq, 2 * d) for _ in maps]
    keys = lambda c, i: kc_ref[i] if c == n_lat else kl_ref[i, _rows(c, tk), :]
    vals = lambda c: vct_ref[...] if c == n_lat else vlt_ref[c]

    def accumulate(c, pas):
        vt = vals(c)
        for i in maps:
            sm[i].accumulate(vt, pas[i])

    _pipeline3(n_lat + 1,
               lambda c: [_nt_dot(keys(c, i), q_ref[i]) for i in maps],
               lambda c, ss: [sm[i].probs(ss[i]) for i in maps],
               accumulate)

    lam = (jnp.exp(jnp.sum(lq1_ref[...] * lk1_ref[...], axis=1, keepdims=True))
           - jnp.exp(jnp.sum(lq2_ref[...] * lk2_ref[...], axis=1, keepdims=True)) + lambda_init)
    o = (sm[0].result() - lam * sm[1].result()).T
    ms = jnp.mean(o * o, axis=-1, keepdims=True)
    o_ref[...] = ((o * lax.rsqrt(ms + EPS) * sg_ref[...]) * (1.0 - lambda_init)).astype(o_ref.dtype)


def diff_attention(rot_q, rot_lat, proj_lat, rot_ctx, proj_ctx, lam_params, subln_g, lambda_init, tq, tk):
    mq = rot_q.shape[1]
    nq = mq // BATCH // tq
    with_lat = rot_lat is not None
    pair = lambda base, n: pl.BlockSpec((2, n, HEAD_DIM), lambda b, h, i: (base // 2 + h, b, 0))
    in_specs = [pl.BlockSpec((2, tq, HEAD_DIM), lambda b, h, i: (R_QB // 2 + h, b * nq + i, 0))]
    args = [rot_q]
    if with_lat:
        in_specs += [pair(R_KB, SEQ), pair(P_VB, SEQ)]
        args += [rot_lat, proj_lat]
    in_specs += [pair(R_KB, CTX_LEN), pair(P_VB, CTX_LEN)]
    args += [rot_ctx, proj_ctx]
    vec = lambda n: pl.BlockSpec((1, n), lambda b, h, i: (0, 0))
    in_specs += [vec(HEAD_DIM)] * 4 + [vec(2 * HEAD_DIM)]
    args += [p.reshape(1, HEAD_DIM) for p in lam_params] + [subln_g.reshape(1, 2 * HEAD_DIM)]
    scratch = [pltpu.VMEM((SEQ // tk, 2 * HEAD_DIM, tk), BF16)] if with_lat else []
    scratch += [pltpu.VMEM((2 * HEAD_DIM, CTX_LEN), BF16)]
    return pl.pallas_call(
        functools.partial(_diff_kernel, with_lat=with_lat, tk=tk, lambda_init=lambda_init),
        grid=(BATCH, B_HEADS, nq),
        in_specs=in_specs,
        out_specs=pl.BlockSpec((tq, 2 * HEAD_DIM), lambda b, h, i: (b * nq + i, h)),
        out_shape=jax.ShapeDtypeStruct((mq, B_HEADS * 2 * HEAD_DIM), BF16),
        scratch_shapes=scratch,
        compiler_params=_params(("arbitrary", "arbitrary", "arbitrary")),
        name="diff_attention",
    )(*args)


NA_QROWS = 4
NA_CHUNK = NA_QROWS * GRID_W
NA_BAND_CHUNKS = 3
N_DR = 2 * NA_ROWS - 1
NA_ONLY_A, NA_ONLY_B, NA_NEITHER = N_DR - 1, 2 * N_DR - 1, 3 * N_DR - 1


def _na_tile_index(key_row, row_a):
    first = lambda r: min(max(r - NA_ROWS // 2, 0), ROWS - NA_ROWS)
    va, vb = (first(r) <= key_row < first(r) + NA_ROWS for r in (row_a, row_a + 1))
    da = key_row - row_a + (NA_ROWS - 1)
    return da - 1 if (va and vb) else NA_ONLY_A + da if va else NA_ONLY_B - 1 + da if vb else NA_NEITHER


def _na_kernel(q_ref, k_ref, v_ref, kc_ref, vc_ref, tab_ref, o_ref, vt_ref, vct_ref):
    n_chunks = vt_ref.shape[0]
    chunk = lambda c: _rows(c, NA_CHUNK)
    band_start = lambda blk: min(max(blk - 1, 0), n_chunks - NA_BAND_CHUNKS)

    _transpose_to(vct_ref, vc_ref[...])

    def transpose_chunk(c, carry):
        _transpose_to(vt_ref.at[c], v_ref[chunk(c), :])
        return carry
    lax.fori_loop(0, n_chunks, transpose_chunk, 0)

    def scores(blk):
        row0 = blk * NA_QROWS
        cb = band_start(blk)
        q = q_ref[chunk(blk), :]
        s = []
        for c in range(NA_BAND_CHUNKS):
            tiles = []
            for i in range(NA_QROWS):
                key_row = (cb + c) * NA_QROWS + i
                tiles.append(jnp.concatenate(
                    [tab_ref[_na_tile_index(key_row, row0 + 2 * j)] for j in range(NA_QROWS // 2)], axis=1))
            s.append(_nt_dot(k_ref[chunk(cb + c), :], q) * SCALE_LOG2 + jnp.concatenate(tiles, axis=0))
        s.append(_nt_dot(kc_ref[...], q) * SCALE_LOG2)
        return s

    def probs(blk, s):
        m = functools.reduce(jnp.maximum, [jnp.max(x, axis=0, keepdims=True) for x in s])
        p = [jnp.exp2(x - m) for x in s]
        l = functools.reduce(jnp.add, [jnp.sum(x, axis=0, keepdims=True) for x in p])
        return [x.astype(BF16) for x in p], 1.0 / l

    def output(blk, p_linv):
        p, linv = p_linv
        cb = band_start(blk)
        o_t = jnp.dot(vct_ref[...], p[NA_BAND_CHUNKS], preferred_element_type=F32)
        for c in range(NA_BAND_CHUNKS):
            o_t = o_t + jnp.dot(vt_ref[cb + c], p[c], preferred_element_type=F32)
        o_ref[chunk(blk), :] = (o_t * linv).T.astype(o_ref.dtype)

    _pipeline3(n_chunks, scores, probs, output)


def neighbourhood_attention(proj_lat, proj_ctx, bias_table):
    assert CTX_LEN == NA_CHUNK
    lat = lambda base: pl.BlockSpec((None, SEQ, HEAD_DIM), lambda b, h: (base + h, b, 0))
    ctx = lambda base: pl.BlockSpec((None, CTX_LEN, HEAD_DIM), lambda b, h: (base + h, b, 0))
    return pl.pallas_call(
        _na_kernel,
        grid=(BATCH, C_HEADS),
        in_specs=[lat(0), lat(C_HEADS), lat(2 * C_HEADS), ctx(0), ctx(C_HEADS),
                  pl.BlockSpec((None, 3 * N_DR, GRID_W, 2 * GRID_W), lambda b, h: (h, 0, 0, 0))],
        out_specs=pl.BlockSpec((SEQ, HEAD_DIM), lambda b, h: (b, h)),
        out_shape=jax.ShapeDtypeStruct((BATCH * SEQ, C_HEADS * HEAD_DIM), BF16),
        scratch_shapes=[pltpu.VMEM((SEQ // NA_CHUNK, HEAD_DIM, NA_CHUNK), BF16),
                        pltpu.VMEM((HEAD_DIM, CTX_LEN), BF16)],
        compiler_params=_params(("arbitrary", "arbitrary")),
        name="neighbourhood_attention",
    )(proj_lat, proj_lat, proj_lat, proj_ctx, proj_ctx, bias_table)


def _rope_tables():
    t = jnp.arange(SEQ, dtype=jnp.int32)
    row = (t // GRID_W).astype(F32)
    col = (t % GRID_W).astype(F32)
    n_freq = HEAD_DIM // 4
    freqs = ROPE_THETA ** (-jnp.arange(n_freq, dtype=F32) / n_freq)
    ar, ac = row[:, None] * freqs, col[:, None] * freqs
    cos_t = jnp.concatenate([jnp.cos(ar), jnp.cos(ar), jnp.cos(ac), jnp.cos(ac)], axis=1)
    sin_t = jnp.concatenate([-jnp.sin(ar), jnp.sin(ar), -jnp.sin(ac), jnp.sin(ac)], axis=1)
    return cos_t, sin_t


def _na_bias_table(rel_bias):
    col = jnp.arange(GRID_W, dtype=jnp.int32)
    c0 = jnp.clip(col - NA_COLS // 2, 0, GRID_W - NA_COLS)
    key, qry = col[:, None], col[None, :]
    dc = key - qry + (NA_COLS - 1)
    valid = (key >= c0[None, :]) & (key < c0[None, :] + NA_COLS)
    t = rel_bias[:, :, jnp.clip(dc, 0, 2 * NA_COLS - 2)].astype(F32) * LOG2E
    t = jnp.where(valid[None, None], t, NEG)
    off = jnp.full_like(t, NEG)
    both = jnp.concatenate([t[:, 1:], t[:, :-1]], axis=-1)
    only_a = jnp.concatenate([t, off], axis=-1)
    only_b = jnp.concatenate([off, t], axis=-1)
    neither = jnp.concatenate([off[:, :1], off[:, :1]], axis=-1)
    return jnp.concatenate([both, only_a, only_b, neither], axis=1)


def kernel(x, c, ctx, c_ctx, ada_w, ada_b, norm1_g, norm2_g, w_in_even, w_out_even, a_q_norm, a_k_norm,
           b_lambda_q1, b_lambda_k1, b_lambda_q2, b_lambda_k2, b_subln_g, w_in_odd, w_out_odd, na_rel_bias,
           mlp_w1, mlp_w2, final_g):
    TM = 1024
    TC = BATCH * CTX_LEN
    TN = 512
    lat_row = lambda m: m // (SEQ // TM)
    lat_row_n = lambda m: m // (SEQ // TN)
    ctx_row = lambda m: 2

    craw = jnp.concatenate([c, c_ctx[None], jnp.zeros((5, D_MODEL), F32)], axis=0)
    mod = ada_modulation(craw, ada_w, ada_b).reshape(DEPTH, 8, 6, 1, D_MODEL)

    h = x.reshape(BATCH * SEQ, D_MODEL)
    hc = ctx.reshape(TC, D_MODEL)
    cos_t, sin_t = _rope_tables()
    one_t, zero_t = jnp.ones((TC, HEAD_DIM), F32), jnp.zeros((TC, HEAD_DIM), F32)

    mod0 = mod[0]
    w_in = w_in_even[0].astype(BF16)
    w_out = w_out_even[0].astype(BF16)
    ones = jnp.ones((HEAD_DIM,), F32)
    gains = jnp.stack([a_q_norm[0] * SCALE_LOG2] * A_HEADS + [a_k_norm[0]] * A_KV_HEADS
                      + [ones * SCALE_LOG2] * (2 * B_HEADS) + [ones] * (2 * B_HEADS)).reshape(N_ROT, 1, HEAD_DIM)
    lam_params = (b_lambda_q1[0], b_lambda_k1[0], b_lambda_q2[0], b_lambda_k2[0])
    lambda_init = 0.8 - 0.6 * math.exp(-0.3 * 0)

    u = norm_modulate(h, norm1_g[0], mod0, 0, 1, lat_row_n, TN)
    uc = norm_modulate(hc, norm1_g[0], mod0, 0, 1, ctx_row, TC)
    proj = project_heads(u, w_in, TM, 1024)
    proj_c = project_heads(uc, w_in, TC, 1024)
    rot = prep_qk(proj, cos_t, sin_t, gains, 2048)
    rot_c = prep_qk(proj_c, one_t, zero_t, gains, TC)

    a_lat = gqa_attention(rot, proj, rot, proj, rot_c, proj_c, 128, 512)
    b_lat = diff_attention(rot, rot, proj, rot_c, proj_c, lam_params, b_subln_g[0], lambda_init, 512, 512)
    a_ctx = gqa_attention(rot_c, proj_c, None, None, rot_c, proj_c, 128, 512)
    b_ctx = diff_attention(rot_c, None, None, rot_c, proj_c, lam_params, b_subln_g[0], lambda_init, 256, 512)

    h = out_project_residual([a_lat, b_lat], w_out, h, mod0, 2, lat_row, TM, 1024)
    hc = out_project_residual([a_ctx, b_ctx], w_out, hc, mod0, 2, ctx_row, TC, 1024)
    w1 = mlp_w1[0].astype(BF16)
    w2 = mlp_w2[0].astype(BF16)
    h = mlp_residual(h, norm2_g[0], mod0, lat_row, w1, w2, final_g, False, TM, 512)
    hc = mlp_residual(hc, norm2_g[0], mod0, ctx_row, w1, w2, final_g, False, TC, 256)

    mod1 = mod[1]
    w_in = w_in_odd[0].astype(BF16)
    u = norm_modulate(h, norm1_g[1], mod1, 0, 1, lat_row_n, TN)
    uc = norm_modulate(hc, norm1_g[1], mod1, 0, 1, ctx_row, TC)
    proj = project_heads(u, w_in, TM, 1024)
    proj_c = project_heads(uc, w_in[:, C_HEADS * HEAD_DIM:], TC, 1024)
    attn = neighbourhood_attention(proj, proj_c, _na_bias_table(na_rel_bias[0]))
    h = out_project_residual([attn], w_out_odd[0].astype(BF16), h, mod1, 2, lat_row, TM, 1024)
    out = mlp_residual(h, norm2_g[1], mod1, lat_row, mlp_w1[1].astype(BF16), mlp_w2[1].astype(BF16),
                       final_g, True, TM, 512)
    return out.reshape(BATCH, SEQ, D_MODEL)
```

```python
import functools
import math

import jax
import jax.numpy as jnp
from jax import lax
from jax.experimental import pallas as pl
from jax.experimental.pallas import tpu as pltpu

D_MODEL = 4096
BATCH = 2
SEQ = 8192
DEPTH = 2
GRID_W = 64
ROWS = SEQ // GRID_W
CTX_LEN = 256
HEAD_DIM = 128
ROPE_THETA = 10000.0
EPS = 1e-6
A_HEADS = 16
A_KV_HEADS = 4
A_GROUP = A_HEADS // A_KV_HEADS
B_HEADS = 8
C_HEADS = 32
NA_ROWS = 8
NA_COLS = 16
MLP_HIDDEN = 4 * D_MODEL
EVEN_IN = 9216
SCALE = HEAD_DIM ** -0.5
LOG2E = 1.4426950408889634
SCALE_LOG2 = SCALE * LOG2E

P_QA, P_KA, P_VA, P_QB, P_KB, P_VB = 0, 16, 20, 24, 40, 56
N_PROJ_EVEN = EVEN_IN // HEAD_DIM
R_QA, R_KA, R_QB, R_KB = 0, 16, 20, 36
N_ROT = 52
N_NORMED = 20

VMEM_LIMIT = 62 * 1024 * 1024
NEG = -1e30

F32 = jnp.float32
BF16 = jnp.bfloat16


def _params(sem, vmem=VMEM_LIMIT):
    return pltpu.CompilerParams(dimension_semantics=sem, vmem_limit_bytes=vmem)


def _ada_kernel(c_ref, w_ref, b_ref, o_ref):
    x = c_ref[...]
    cond = x / (1.0 + jnp.exp(-x))
    o_ref[...] = jnp.dot(cond.astype(BF16), w_ref[...].astype(BF16),
                         preferred_element_type=F32) + b_ref[...]


def ada_modulation(craw, ada_w, ada_b):
    tn = 512
    n6 = 6 * D_MODEL
    return pl.pallas_call(
        _ada_kernel,
        grid=(DEPTH, n6 // tn),
        in_specs=[pl.BlockSpec((8, D_MODEL), lambda i, n: (0, 0)),
                  pl.BlockSpec((None, D_MODEL, tn), lambda i, n: (i, 0, n)),
                  pl.BlockSpec((None, 1, tn), lambda i, n: (i, 0, n))],
        out_specs=pl.BlockSpec((None, 8, tn), lambda i, n: (i, 0, n)),
        out_shape=jax.ShapeDtypeStruct((DEPTH, 8, n6), F32),
        compiler_params=_params(("arbitrary", "arbitrary")),
        name="ada_modulation",
    )(craw, ada_w, ada_b.reshape(DEPTH, 1, n6))


ROW_CHUNK = 16
ROW_UNROLL = 4


def _row_chunks(n_rows, fn):
    def body(i, carry):
        fn(pl.ds(pl.multiple_of(i * ROW_CHUNK, ROW_CHUNK), ROW_CHUNK))
        return carry
    lax.fori_loop(0, n_rows // ROW_CHUNK, body, 0, unroll=ROW_UNROLL)


def _modulate_into(dst_ref, h_ref, g_ref, sh_ref, sc_ref):
    g, up, sh = g_ref[...], 1.0 + sc_ref[...], sh_ref[...]

    def rows_fn(rows):
        x = h_ref[rows, :]
        ms = jnp.mean(x * x, axis=-1, keepdims=True)
        dst_ref[rows, :] = ((x * lax.rsqrt(ms + EPS) * g) * up + sh).astype(dst_ref.dtype)

    _row_chunks(h_ref.shape[0], rows_fn)


def _normmod_kernel(h_ref, g_ref, sh_ref, sc_ref, o_ref):
    _modulate_into(o_ref, h_ref, g_ref, sh_ref, sc_ref)


def _mod_spec(which, row_of):
    return pl.BlockSpec((None, None, 1, D_MODEL), lambda m, *_: (row_of(m), which, 0, 0))


def norm_modulate(h, g, mod, which_shift, which_scale, row_of, tm):
    m = h.shape[0]
    return pl.pallas_call(
        _normmod_kernel,
        grid=(m // tm,),
        in_specs=[pl.BlockSpec((tm, D_MODEL), lambda i: (i, 0)),
                  pl.BlockSpec((1, D_MODEL), lambda i: (0, 0)),
                  _mod_spec(which_shift, row_of), _mod_spec(which_scale, row_of)],
        out_specs=pl.BlockSpec((tm, D_MODEL), lambda i: (i, 0)),
        out_shape=jax.ShapeDtypeStruct((m, D_MODEL), BF16),
        compiler_params=_params(("arbitrary",)),
        name="norm_modulate",
    )(h, g.reshape(1, D_MODEL), mod, mod)


def _proj_kernel(x_ref, w_ref, o_ref):
    acc = jnp.dot(x_ref[...], w_ref[...], preferred_element_type=F32)
    for j in range(o_ref.shape[0]):
        o_ref[j] = acc[:, j * HEAD_DIM:(j + 1) * HEAD_DIM].astype(o_ref.dtype)


def project_heads(x, w, tm, tn):
    m, k = x.shape
    n = w.shape[1]
    return pl.pallas_call(
        _proj_kernel,
        grid=(n // tn, m // tm),
        in_specs=[pl.BlockSpec((tm, k), lambda j, i: (i, 0)),
                  pl.BlockSpec((k, tn), lambda j, i: (0, j))],
        out_specs=pl.BlockSpec((tn // HEAD_DIM, tm, HEAD_DIM), lambda j, i: (j, i, 0)),
        out_shape=jax.ShapeDtypeStruct((n // HEAD_DIM, m, HEAD_DIM), BF16),
        compiler_params=_params(("arbitrary", "arbitrary")),
        name="project_heads",
    )(x, w)


def _outproj_kernel(*refs, n_x):
    x_refs, (w_ref, h_ref, g_ref, o_ref) = refs[:n_x], refs[n_x:]
    acc = None
    off = 0
    for x_ref in x_refs:
        kk = x_ref.shape[1]
        part = jnp.dot(x_ref[...], w_ref[off:off + kk, :], preferred_element_type=F32)
        acc = part if acc is None else acc + part
        off += kk
    o_ref[...] = h_ref[...] + g_ref[...] * acc


def out_project_residual(xs, w, h, mod, which_gate, row_of, tm, tn):
    m = h.shape[0]
    k = w.shape[0]
    in_specs = [pl.BlockSpec((tm, x.shape[1]), lambda j, i: (i, 0)) for x in xs]
    in_specs += [pl.BlockSpec((k, tn), lambda j, i: (0, j)),
                 pl.BlockSpec((tm, tn), lambda j, i: (i, j)),
                 pl.BlockSpec((None, None, 1, tn), lambda j, i: (row_of(i), which_gate, 0, j))]
    return pl.pallas_call(
        functools.partial(_outproj_kernel, n_x=len(xs)),
        grid=(D_MODEL // tn, m // tm),
        in_specs=in_specs,
        out_specs=pl.BlockSpec((tm, tn), lambda j, i: (i, j)),
        out_shape=jax.ShapeDtypeStruct((m, D_MODEL), F32),
        compiler_params=_params(("arbitrary", "arbitrary")),
        name="out_project_residual",
    )(*xs, w, h, mod)


def _mlp_kernel(h_ref, gn_ref, sh_ref, sc_ref, gate_ref, w1_ref, w2_ref, fg_ref, o_ref, xs_ref, hid_ref, *,
                final_norm):
    j = pl.program_id(1)

    @pl.when(j == 0)
    def _():
        _modulate_into(xs_ref, h_ref, gn_ref, sh_ref, sc_ref)
        o_ref[...] = jnp.zeros_like(o_ref)
        hid_ref[...] = jnp.zeros_like(hid_ref)

    o_ref[...] = jnp.dot(hid_ref[...], w2_ref[...], preferred_element_type=F32) + o_ref[...]
    hid = jnp.dot(xs_ref[...], w1_ref[...], preferred_element_type=F32)
    hid_ref[...] = jnp.square(jnp.maximum(hid, 0.0)).astype(BF16)

    @pl.when(j == pl.num_programs(1) - 1)
    def _():
        gate, fg = gate_ref[...], fg_ref[...]

        def rows_fn(rows):
            y = h_ref[rows, :] + gate * o_ref[rows, :]
            if final_norm:
                ms = jnp.mean(y * y, axis=-1, keepdims=True)
                y = y * lax.rsqrt(ms + EPS) * fg
            o_ref[rows, :] = y

        _row_chunks(h_ref.shape[0], rows_fn)


def mlp_residual(h, gn, mod, row_of, w1, w2, final_g, final_norm, tm, th):
    m = h.shape[0]
    nj = MLP_HIDDEN // th
    once = pl.Buffered(1)
    row = lambda i, j: (i, 0)
    return pl.pallas_call(
        functools.partial(_mlp_kernel, final_norm=final_norm),
        grid=(m // tm, nj + 1),
        in_specs=[pl.BlockSpec((tm, D_MODEL), row, pipeline_mode=once),
                  pl.BlockSpec((1, D_MODEL), lambda i, j: (0, 0)),
                  _mod_spec(3, row_of), _mod_spec(4, row_of), _mod_spec(5, row_of),
                  pl.BlockSpec((D_MODEL, th), lambda i, j: (0, jnp.minimum(j, nj - 1))),
                  pl.BlockSpec((th, D_MODEL), lambda i, j: (jnp.maximum(j - 1, 0), 0)),
                  pl.BlockSpec((1, D_MODEL), lambda i, j: (0, 0))],
        out_specs=pl.BlockSpec((tm, D_MODEL), row, pipeline_mode=once),
        out_shape=jax.ShapeDtypeStruct((m, D_MODEL), F32),
        scratch_shapes=[pltpu.VMEM((tm, D_MODEL), BF16), pltpu.VMEM((tm, th), BF16)],
        compiler_params=_params(("arbitrary", "arbitrary")),
        name="mlp_residual",
    )(h, gn.reshape(1, D_MODEL), mod, mod, mod, w1, w2, final_g.reshape(1, D_MODEL))


def _rotary_partner(lane):
    return jnp.where(lane % 64 < 32, lane + 32, lane - 32)


def _prep_kernel(x_ref, c_ref, s_ref, g_ref, pg_ref, perm_ref, o_ref):
    n = x_ref.shape[0]
    d = x_ref.shape[2]
    normed = pl.program_id(1) < N_NORMED // n
    c, s = c_ref[...], s_ref[...]
    xp = jnp.dot(jnp.concatenate([x_ref[i] for i in range(n)], axis=1), perm_ref[...], preferred_element_type=F32)
    for i in range(n):
        x = x_ref[i].astype(F32)
        ms = jnp.mean(x * x, axis=-1, keepdims=True)
        inv = jnp.where(normed, lax.rsqrt(ms + EPS), 1.0)
        o_ref[i] = ((x * (g_ref[i] * c) + xp[:, i * d:(i + 1) * d] * (pg_ref[i] * s)) * inv).astype(o_ref.dtype)


def prep_qk(proj, cos_t, sin_t, gains, tm):
    m = proj.shape[1]
    nt = cos_t.shape[0] // tm
    pair = 2
    assert N_NORMED % pair == 0 and N_ROT % pair == 0 and P_QB % pair == 0
    src = lambda j: jnp.where(j < N_NORMED // pair, j, j + (P_QB - N_NORMED) // pair)
    lane = jnp.arange(pair * HEAD_DIM, dtype=jnp.int32)
    partner = _rotary_partner(lane % HEAD_DIM) + (lane // HEAD_DIM) * HEAD_DIM
    perm = (lane[:, None] == partner[None, :]).astype(BF16)
    g = gains.reshape(N_ROT // pair, pair, 1, HEAD_DIM)
    pg = gains[..., partner[:HEAD_DIM]].reshape(N_ROT // pair, pair, 1, HEAD_DIM)
    vec = pl.BlockSpec((None, pair, 1, HEAD_DIM), lambda i, j: (j, 0, 0, 0))
    return pl.pallas_call(
        _prep_kernel,
        grid=(m // tm, N_ROT // pair),
        in_specs=[pl.BlockSpec((pair, tm, HEAD_DIM), lambda i, j: (src(j), i, 0)),
                  pl.BlockSpec((tm, HEAD_DIM), lambda i, j: (i % nt, 0)),
                  pl.BlockSpec((tm, HEAD_DIM), lambda i, j: (i % nt, 0)),
                  vec, vec,
                  pl.BlockSpec((pair * HEAD_DIM, pair * HEAD_DIM), lambda i, j: (0, 0))],
        out_specs=pl.BlockSpec((pair, tm, HEAD_DIM), lambda i, j: (j, i, 0)),
        out_shape=jax.ShapeDtypeStruct((N_ROT, m, HEAD_DIM), BF16),
        compiler_params=_params(("arbitrary", "arbitrary")),
        name="prep_qk",
    )(proj, cos_t, sin_t, g, pg, perm)


def _nt_dot(a, b):
    return lax.dot_general(a, b, (((1,), (1,)), ((), ())), preferred_element_type=F32)


def _rows(c, size):
    return pl.ds(c * size, size) if isinstance(c, int) else pl.ds(pl.multiple_of(c * size, size), size)


def _transpose_to(dst_ref, src):
    dst_ref[...] = src.astype(F32).T.astype(dst_ref.dtype)


def _pipeline3(n, stage1, stage2, stage3):
    if n == 1:
        stage3(0, stage2(0, stage1(0)))
        return
    x = {0: stage1(0), 1: stage1(1)}
    y = {0: stage2(0, x.pop(0))}
    for i in range(1, n):
        stage3(i - 1, y.pop(i - 1))
        if i + 1 < n:
            x[i + 1] = stage1(i + 1)
        y[i] = stage2(i, x.pop(i))
    stage3(n - 1, y.pop(n - 1))


class _OnlineSoftmax:
    def __init__(self, n_queries, dv):
        self.m = jnp.full((1, n_queries), NEG, F32)
        self.l = jnp.zeros((1, n_queries), F32)
        self.acc = jnp.zeros((dv, n_queries), F32)

    def probs(self, s):
        m_new = jnp.maximum(self.m, jnp.max(s, axis=0, keepdims=True))
        alpha = jnp.exp2(self.m - m_new)
        p = jnp.exp2(s - m_new)
        self.l = alpha * self.l + jnp.sum(p, axis=0, keepdims=True)
        self.m = m_new
        return p.astype(BF16), alpha

    def accumulate(self, vt, p_alpha):
        p, alpha = p_alpha
        self.acc = alpha * self.acc + jnp.dot(vt, p, preferred_element_type=F32)

    def result(self):
        return self.acc * (1.0 / self.l)


def _gqa_kernel(*refs, with_lat, tk):
    if with_lat:
        q_ref, kl_ref, vl_ref, kc_ref, vc_ref, o_ref, vlt_ref, vct_ref = refs
    else:
        q_ref, kc_ref, vc_ref, o_ref, vct_ref = refs
    g, tq, d = q_ref.shape
    n_lat = vlt_ref.shape[0] if with_lat else 0

    @pl.when(pl.program_id(2) == 0)
    def _():
        _transpose_to(vct_ref, vc_ref[...])
        if with_lat:
            def body(c, carry):
                _transpose_to(vlt_ref.at[c], vl_ref[_rows(c, tk), :])
                return carry
            lax.fori_loop(0, n_lat, body, 0)

    q = q_ref[...].reshape(g * tq, d)
    sm = _OnlineSoftmax(g * tq, d)
    keys = lambda c: kc_ref[...] if c == n_lat else kl_ref[_rows(c, tk), :]
    vals = lambda c: vct_ref[...] if c == n_lat else vlt_ref[c]
    _pipeline3(n_lat + 1,
               lambda c: _nt_dot(keys(c), q),
               lambda c, s: sm.probs(s),
               lambda c, pa: sm.accumulate(vals(c), pa))
    o = sm.result().T
    for h in range(g):
        o_ref[:, h * d:(h + 1) * d] = o[h * tq:(h + 1) * tq].astype(o_ref.dtype)


def gqa_attention(rot_q, proj_q, rot_lat, proj_lat, rot_ctx, proj_ctx, tq, tk):
    mq = rot_q.shape[1]
    nq = mq // BATCH // tq
    with_lat = rot_lat is not None
    q_spec = pl.BlockSpec((A_GROUP, tq, HEAD_DIM), lambda b, g, i: (g, b * nq + i, 0))
    kv = lambda base, n: pl.BlockSpec((None, n, HEAD_DIM), lambda b, g, i: (base + g, b, 0))
    in_specs, args = [q_spec], [rot_q]
    if with_lat:
        in_specs += [kv(R_KA, SEQ), kv(P_VA, SEQ)]
        args += [rot_lat, proj_lat]
    in_specs += [kv(R_KA, CTX_LEN), kv(P_VA, CTX_LEN)]
    args += [rot_ctx, proj_ctx]
    scratch = [pltpu.VMEM((SEQ // tk, HEAD_DIM, tk), BF16)] if with_lat else []
    scratch += [pltpu.VMEM((HEAD_DIM, CTX_LEN), BF16)]
    return pl.pallas_call(
        functools.partial(_gqa_kernel, with_lat=with_lat, tk=tk),
        grid=(BATCH, A_KV_HEADS, nq),
        in_specs=in_specs,
        out_specs=pl.BlockSpec((tq, A_GROUP * HEAD_DIM), lambda b, g, i: (b * nq + i, g)),
        out_shape=jax.ShapeDtypeStruct((mq, A_HEADS * HEAD_DIM), BF16),
        scratch_shapes=scratch,
        compiler_params=_params(("arbitrary", "arbitrary", "arbitrary")),
        name="gqa_attention",
    )(*args)


def _diff_kernel(*refs, with_lat, tk, lambda_init):
    if with_lat:
        (q_ref, kl_ref, vl_ref, kc_ref, vc_ref, lq1_ref, lk1_ref, lq2_ref, lk2_ref, sg_ref,
         o_ref, vlt_ref, vct_ref) = refs
    else:
        q_ref, kc_ref, vc_ref, lq1_ref, lk1_ref, lq2_ref, lk2_ref, sg_ref, o_ref, vct_ref = refs
    tq, d = q_ref.shape[1:]
    n_lat = vlt_ref.shape[0] if with_lat else 0

    @pl.when(pl.program_id(2) == 0)
    def _():
        for half in range(2):
            _transpose_to(vct_ref.at[half * d:(half + 1) * d, :], vc_ref[half])
        if with_lat:
            def body(c, carry):
                for half in range(2):
                    _transpose_to(vlt_ref.at[c, half * d:(half + 1) * d, :], vl_ref[half, _rows(c, tk), :])
                return carry
            lax.fori_loop(0, n_lat, body, 0)

    maps = (0, 1)
    sm = [_OnlineSoftmax(tq, 2 * d) for _ in maps]
    keys = lambda c, i: kc_ref[i] if c == n_lat else kl_ref[i, _rows(c, tk), :]
    vals = lambda c: vct_ref[...] if c == n_lat else vlt_ref[c]

    def accumulate(c, pas):
        vt = vals(c)
        for i in maps:
            sm[i].accumulate(vt, pas[i])

    _pipeline3(n_lat + 1,
               lambda c: [_nt_dot(keys(c, i), q_ref[i]) for i in maps],
               lambda c, ss: [sm[i].probs(ss[i]) for i in maps],
               accumulate)

    lam = (jnp.exp(jnp.sum(lq1_ref[...] * lk1_ref[...], axis=1, keepdims=True))
           - jnp.exp(jnp.sum(lq2_ref[...] * lk2_ref[...], axis=1, keepdims=True)) + lambda_init)
    o = (sm[0].result() - lam * sm[1].result()).T
    ms = jnp.mean(o * o, axis=-1, keepdims=True)
    o_ref[...] = ((o * lax.rsqrt(ms + EPS) * sg_ref[...]) * (1.0 - lambda_init)).astype(o_ref.dtype)


def diff_attention(rot_q, rot_lat, proj_lat, rot_ctx, proj_ctx, lam_params, subln_g, lambda_init, tq, tk):
    mq = rot_q.shape[1]
    nq = mq // BATCH // tq
    with_lat = rot_lat is not None
    pair = lambda base, n: pl.BlockSpec((2, n, HEAD_DIM), lambda b, h, i: (base // 2 + h, b, 0))
    in_specs = [pl.BlockSpec((2, tq, HEAD_DIM), lambda b, h, i: (R_QB // 2 + h, b * nq + i, 0))]
    args = [rot_q]
    if with_lat:
        in_specs += [pair(R_KB, SEQ), pair(P_VB, SEQ)]
        args += [rot_lat, proj_lat]
    in_specs += [pair(R_KB, CTX_LEN), pair(P_VB, CTX_LEN)]
    args += [rot_ctx, proj_ctx]
    vec = lambda n: pl.BlockSpec((1, n), lambda b, h, i: (0, 0))
    in_specs += [vec(HEAD_DIM)] * 4 + [vec(2 * HEAD_DIM)]
    args += [p.reshape(1, HEAD_DIM) for p in lam_params] + [subln_g.reshape(1, 2 * HEAD_DIM)]
    scratch = [pltpu.VMEM((SEQ // tk, 2 * HEAD_DIM, tk), BF16)] if with_lat else []
    scratch += [pltpu.VMEM((2 * HEAD_DIM, CTX_LEN), BF16)]
    return pl.pallas_call(
        functools.partial(_diff_kernel, with_lat=with_lat, tk=tk, lambda_init=lambda_init),
        grid=(BATCH, B_HEADS, nq),
        in_specs=in_specs,
        out_specs=pl.BlockSpec((tq, 2 * HEAD_DIM), lambda b, h, i: (b * nq + i, h)),
        out_shape=jax.ShapeDtypeStruct((mq, B_HEADS * 2 * HEAD_DIM), BF16),
        scratch_shapes=scratch,
        compiler_params=_params(("arbitrary", "arbitrary", "arbitrary")),
        name="diff_attention",
    )(*args)


NA_QROWS = 4
NA_CHUNK = NA_QROWS * GRID_W
NA_BAND_CHUNKS = 3
N_DR = 2 * NA_ROWS - 1
NA_ONLY_A, NA_ONLY_B, NA_NEITHER = N_DR - 1, 2 * N_DR - 1, 3 * N_DR - 1


def _na_tile_index(key_row, row_a):
    first = lambda r: min(max(r - NA_ROWS // 2, 0), ROWS - NA_ROWS)
    va, vb = (first(r) <= key_row < first(r) + NA_ROWS for r in (row_a, row_a + 1))
    da = key_row - row_a + (NA_ROWS - 1)
    return da - 1 if (va and vb) else NA_ONLY_A + da if va else NA_ONLY_B - 1 + da if vb else NA_NEITHER


def _na_kernel(q_ref, k_ref, v_ref, kc_ref, vc_ref, tab_ref, o_ref, vt_ref, vct_ref):
    n_chunks = vt_ref.shape[0]
    chunk = lambda c: _rows(c, NA_CHUNK)
    band_start = lambda blk: min(max(blk - 1, 0), n_chunks - NA_BAND_CHUNKS)

    _transpose_to(vct_ref, vc_ref[...])

    def transpose_chunk(c, carry):
        _transpose_to(vt_ref.at[c], v_ref[chunk(c), :])
        return carry
    lax.fori_loop(0, n_chunks, transpose_chunk, 0)

    def scores(blk):
        row0 = blk * NA_QROWS
        cb = band_start(blk)
        q = q_ref[chunk(blk), :]
        s = []
        for c in range(NA_BAND_CHUNKS):
            tiles = []
            for i in range(NA_QROWS):
                key_row = (cb + c) * NA_QROWS + i
                tiles.append(jnp.concatenate(
                    [tab_ref[_na_tile_index(key_row, row0 + 2 * j)] for j in range(NA_QROWS // 2)], axis=1))
            s.append(_nt_dot(k_ref[chunk(cb + c), :], q) * SCALE_LOG2 + jnp.concatenate(tiles, axis=0))
        s.append(_nt_dot(kc_ref[...], q) * SCALE_LOG2)
        return s

    def probs(blk, s):
        m = functools.reduce(jnp.maximum, [jnp.max(x, axis=0, keepdims=True) for x in s])
        p = [jnp.exp2(x - m) for x in s]
        l = functools.reduce(jnp.add, [jnp.sum(x, axis=0, keepdims=True) for x in p])
        return [x.astype(BF16) for x in p], 1.0 / l

    def output(blk, p_linv):
        p, linv = p_linv
        cb = band_start(blk)
        o_t = jnp.dot(vct_ref[...], p[NA_BAND_CHUNKS], preferred_element_type=F32)
        for c in range(NA_BAND_CHUNKS):
            o_t = o_t + jnp.dot(vt_ref[cb + c], p[c], preferred_element_type=F32)
        o_ref[chunk(blk), :] = (o_t * linv).T.astype(o_ref.dtype)

    _pipeline3(n_chunks, scores, probs, output)


def neighbourhood_attention(proj_lat, proj_ctx, bias_table):
    assert CTX_LEN == NA_CHUNK
    lat = lambda base: pl.BlockSpec((None, SEQ, HEAD_DIM), lambda b, h: (base + h, b, 0))
    ctx = lambda base: pl.BlockSpec((None, CTX_LEN, HEAD_DIM), lambda b, h: (base + h, b, 0))
    return pl.pallas_call(
        _na_kernel,
        grid=(BATCH, C_HEADS),
        in_specs=[lat(0), lat(C_HEADS), lat(2 * C_HEADS), ctx(0), ctx(C_HEADS),
                  pl.BlockSpec((None, 3 * N_DR, GRID_W, 2 * GRID_W), lambda b, h: (h, 0, 0, 0))],
        out_specs=pl.BlockSpec((SEQ, HEAD_DIM), lambda b, h: (b, h)),
        out_shape=jax.ShapeDtypeStruct((BATCH * SEQ, C_HEADS * HEAD_DIM), BF16),
        scratch_shapes=[pltpu.VMEM((SEQ // NA_CHUNK, HEAD_DIM, NA_CHUNK), BF16),
                        pltpu.VMEM((HEAD_DIM, CTX_LEN), BF16)],
        compiler_params=_params(("arbitrary", "arbitrary")),
        name="neighbourhood_attention",
    )(proj_lat, proj_lat, proj_lat, proj_ctx, proj_ctx, bias_table)


def _rope_tables():
    t = jnp.arange(SEQ, dtype=jnp.int32)
    row = (t // GRID_W).astype(F32)
    col = (t % GRID_W).astype(F32)
    n_freq = HEAD_DIM // 4
    freqs = ROPE_THETA ** (-jnp.arange(n_freq, dtype=F32) / n_freq)
    ar, ac = row[:, None] * freqs, col[:, None] * freqs
    cos_t = jnp.concatenate([jnp.cos(ar), jnp.cos(ar), jnp.cos(ac), jnp.cos(ac)], axis=1)
    sin_t = jnp.concatenate([-jnp.sin(ar), jnp.sin(ar), -jnp.sin(ac), jnp.sin(ac)], axis=1)
    return cos_t, sin_t


def _na_bias_table(rel_bias):
    col = jnp.arange(GRID_W, dtype=jnp.int32)
    c0 = jnp.clip(col - NA_COLS // 2, 0, GRID_W - NA_COLS)
    key, qry = col[:, None], col[None, :]
    dc = key - qry + (NA_COLS - 1)
    valid = (key >= c0[None, :]) & (key < c0[None, :] + NA_COLS)
    t = rel_bias[:, :, jnp.clip(dc, 0, 2 * NA_COLS - 2)].astype(F32) * LOG2E
    t = jnp.where(valid[None, None], t, NEG)
    off = jnp.full_like(t, NEG)
    both = jnp.concatenate([t[:, 1:], t[:, :-1]], axis=-1)
    only_a = jnp.concatenate([t, off], axis=-1)
    only_b = jnp.concatenate([off, t], axis=-1)
    neither = jnp.concatenate([off[:, :1], off[:, :1]], axis=-1)
    return jnp.concatenate([both, only_a, only_b, neither], axis=1)


def kernel(x, c, ctx, c_ctx, ada_w, ada_b, norm1_g, norm2_g, w_in_even, w_out_even, a_q_norm, a_k_norm,
           b_lambda_q1, b_lambda_k1, b_lambda_q2, b_lambda_k2, b_subln_g, w_in_odd, w_out_odd, na_rel_bias,
           mlp_w1, mlp_w2, final_g):
    TM = 1024
    TC = BATCH * CTX_LEN
    TN = 512
    lat_row = lambda m: m // (SEQ // TM)
    lat_row_n = lambda m: m // (SEQ // TN)
    ctx_row = lambda m: 2

    craw = jnp.concatenate([c, c_ctx[None], jnp.zeros((5, D_MODEL), F32)], axis=0)
    mod = ada_modulation(craw, ada_w, ada_b).reshape(DEPTH, 8, 6, 1, D_MODEL)

    h = x.reshape(BATCH * SEQ, D_MODEL)
    hc = ctx.reshape(TC, D_MODEL)
    cos_t, sin_t = _rope_tables()
    one_t, zero_t = jnp.ones((TC, HEAD_DIM), F32), jnp.zeros((TC, HEAD_DIM), F32)

    mod0 = mod[0]
    w_in = w_in_even[0].astype(BF16)
    w_out = w_out_even[0].astype(BF16)
    ones = jnp.ones((HEAD_DIM,), F32)
    gains = jnp.stack([a_q_norm[0] * SCALE_LOG2] * A_HEADS + [a_k_norm[0]] * A_KV_HEADS
                      + [ones * SCALE_LOG2] * (2 * B_HEADS) + [ones] * (2 * B_HEADS)).reshape(N_ROT, 1, HEAD_DIM)
    lam_params = (b_lambda_q1[0], b_lambda_k1[0], b_lambda_q2[0], b_lambda_k2[0])
    lambda_init = 0.8 - 0.6 * math.exp(-0.3 * 0)

    u = norm_modulate(h, norm1_g[0], mod0, 0, 1, lat_row_n, TN)
    uc = norm_modulate(hc, norm1_g[0], mod0, 0, 1, ctx_row, TC)
    proj = project_heads(u, w_in, TM, 1024)
    proj_c = project_heads(uc, w_in, TC, 1024)
    rot = prep_qk(proj, cos_t, sin_t, gains, 2048)
    rot_c = prep_qk(proj_c, one_t, zero_t, gains, TC)

    a_lat = gqa_attention(rot, proj, rot, proj, rot_c, proj_c, 256, 512)
    b_lat = diff_attention(rot, rot, proj, rot_c, proj_c, lam_params, b_subln_g[0], lambda_init, 512, 512)
    a_ctx = gqa_attention(rot_c, proj_c, None, None, rot_c, proj_c, 128, 512)
    b_ctx = diff_attention(rot_c, None, None, rot_c, proj_c, lam_params, b_subln_g[0], lambda_init, 256, 512)

    h = out_project_residual([a_lat, b_lat], w_out, h, mod0, 2, lat_row, TM, 1024)
    hc = out_project_residual([a_ctx, b_ctx], w_out, hc, mod0, 2, ctx_row, TC, 1024)
    w1 = mlp_w1[0].astype(BF16)
    w2 = mlp_w2[0].astype(BF16)
    h = mlp_residual(h, norm2_g[0], mod0, lat_row, w1, w2, final_g, False, TM, 512)
    hc = mlp_residual(hc, norm2_g[0], mod0, ctx_row, w1, w2, final_g, False, TC, 256)

    mod1 = mod[1]
    w_in = w_in_odd[0].astype(BF16)
    u = norm_modulate(h, norm1_g[1], mod1, 0, 1, lat_row_n, TN)
    uc = norm_modulate(hc, norm1_g[1], mod1, 0, 1, ctx_row, TC)
    proj = project_heads(u, w_in, TM, 1024)
    proj_c = project_heads(uc, w_in[:, C_HEADS * HEAD_DIM:], TC, 1024)
    attn = neighbourhood_attention(proj, proj_c, _na_bias_table(na_rel_bias[0]))
    h = out_project_residual([attn], w_out_odd[0].astype(BF16), h, mod1, 2, lat_row, TM, 1024)
    out = mlp_residual(h, norm2_g[1], mod1, lat_row, mlp_w1[1].astype(BF16), mlp_w2[1].astype(BF16),
                       final_g, True, TM, 512)
    return out.reshape(BATCH, SEQ, D_MODEL)
```

```python
import functools
import math

import jax
import jax.numpy as jnp
from jax import lax
from jax.experimental import pallas as pl
from jax.experimental.pallas import tpu as pltpu

D_MODEL = 4096
BATCH = 2
SEQ = 8192
DEPTH = 2
GRID_W = 64
ROWS = SEQ // GRID_W
CTX_LEN = 256
HEAD_DIM = 128
ROPE_THETA = 10000.0
EPS = 1e-6
A_HEADS = 16
A_KV_HEADS = 4
A_GROUP = A_HEADS // A_KV_HEADS
B_HEADS = 8
C_HEADS = 32
NA_ROWS = 8
NA_COLS = 16
MLP_HIDDEN = 4 * D_MODEL
EVEN_IN = 9216
SCALE = HEAD_DIM ** -0.5
LOG2E = 1.4426950408889634
SCALE_LOG2 = SCALE * LOG2E

P_QA, P_KA, P_VA, P_QB, P_KB, P_VB = 0, 16, 20, 24, 40, 56
N_PROJ_EVEN = EVEN_IN // HEAD_DIM
R_QA, R_KA, R_QB, R_KB = 0, 16, 20, 36
N_ROT = 52
N_NORMED = 20

VMEM_LIMIT = 62 * 1024 * 1024
NEG = -1e30

F32 = jnp.float32
BF16 = jnp.bfloat16


def _params(sem, vmem=VMEM_LIMIT):
    return pltpu.CompilerParams(dimension_semantics=sem, vmem_limit_bytes=vmem)


def _ada_kernel(c_ref, w_ref, b_ref, o_ref):
    x = c_ref[...]
    cond = x / (1.0 + jnp.exp(-x))
    o_ref[...] = jnp.dot(cond.astype(BF16), w_ref[...].astype(BF16),
                         preferred_element_type=F32) + b_ref[...]


def ada_modulation(craw, ada_w, ada_b):
    tn = 512
    n6 = 6 * D_MODEL
    return pl.pallas_call(
        _ada_kernel,
        grid=(DEPTH, n6 // tn),
        in_specs=[pl.BlockSpec((8, D_MODEL), lambda i, n: (0, 0)),
                  pl.BlockSpec((None, D_MODEL, tn), lambda i, n: (i, 0, n)),
                  pl.BlockSpec((None, 1, tn), lambda i, n: (i, 0, n))],
        out_specs=pl.BlockSpec((None, 8, tn), lambda i, n: (i, 0, n)),
        out_shape=jax.ShapeDtypeStruct((DEPTH, 8, n6), F32),
        compiler_params=_params(("arbitrary", "arbitrary")),
        name="ada_modulation",
    )(craw, ada_w, ada_b.reshape(DEPTH, 1, n6))


ROW_CHUNK = 16
ROW_UNROLL = 4


def _row_chunks(n_rows, fn):
    def body(i, carry):
        fn(pl.ds(pl.multiple_of(i * ROW_CHUNK, ROW_CHUNK), ROW_CHUNK))
        return carry
    lax.fori_loop(0, n_rows // ROW_CHUNK, body, 0, unroll=ROW_UNROLL)


def _modulate_into(dst_ref, h_ref, g_ref, sh_ref, sc_ref):
    g, up, sh = g_ref[...], 1.0 + sc_ref[...], sh_ref[...]

    def rows_fn(rows):
        x = h_ref[rows, :]
        ms = jnp.mean(x * x, axis=-1, keepdims=True)
        dst_ref[rows, :] = ((x * lax.rsqrt(ms + EPS) * g) * up + sh).astype(dst_ref.dtype)

    _row_chunks(h_ref.shape[0], rows_fn)


def _normmod_kernel(h_ref, g_ref, sh_ref, sc_ref, o_ref):
    _modulate_into(o_ref, h_ref, g_ref, sh_ref, sc_ref)


def _mod_spec(which, row_of):
    return pl.BlockSpec((None, None, 1, D_MODEL), lambda m, *_: (row_of(m), which, 0, 0))


def norm_modulate(h, g, mod, which_shift, which_scale, row_of, tm):
    m = h.shape[0]
    return pl.pallas_call(
        _normmod_kernel,
        grid=(m // tm,),
        in_specs=[pl.BlockSpec((tm, D_MODEL), lambda i: (i, 0)),
                  pl.BlockSpec((1, D_MODEL), lambda i: (0, 0)),
                  _mod_spec(which_shift, row_of), _mod_spec(which_scale, row_of)],
        out_specs=pl.BlockSpec((tm, D_MODEL), lambda i: (i, 0)),
        out_shape=jax.ShapeDtypeStruct((m, D_MODEL), BF16),
        compiler_params=_params(("arbitrary",)),
        name="norm_modulate",
    )(h, g.reshape(1, D_MODEL), mod, mod)


def _proj_kernel(x_ref, w_ref, o_ref):
    acc = jnp.dot(x_ref[...], w_ref[...], preferred_element_type=F32)
    for j in range(o_ref.shape[0]):
        o_ref[j] = acc[:, j * HEAD_DIM:(j + 1) * HEAD_DIM].astype(o_ref.dtype)


def _cast_kernel(x_ref, o_ref):
    o_ref[...] = x_ref[...].astype(o_ref.dtype)


CAST_BLOCK_BYTES = 8 * 1024 * 1024


def cast_layer_bf16(w, layer):
    _, r, c = w.shape
    rows = 1 << int(math.log2(CAST_BLOCK_BYTES // (4 * c)))
    assert r % rows == 0
    return pl.pallas_call(
        _cast_kernel,
        grid=(r // rows,),
        in_specs=[pl.BlockSpec((None, rows, c), lambda i: (layer, i, 0))],
        out_specs=pl.BlockSpec((rows, c), lambda i: (i, 0)),
        out_shape=jax.ShapeDtypeStruct((r, c), BF16),
        compiler_params=_params(("arbitrary",)),
        name="cast_layer_bf16",
    )(w)


def project_heads(x, w, tm, tn, col0=0):
    m, k = x.shape
    n = w.shape[1] - col0
    assert col0 % tn == 0
    return pl.pallas_call(
        _proj_kernel,
        grid=(n // tn, m // tm),
        in_specs=[pl.BlockSpec((tm, k), lambda j, i: (i, 0)),
                  pl.BlockSpec((k, tn), lambda j, i: (0, j + col0 // tn))],
        out_specs=pl.BlockSpec((tn // HEAD_DIM, tm, HEAD_DIM), lambda j, i: (j, i, 0)),
        out_shape=jax.ShapeDtypeStruct((n // HEAD_DIM, m, HEAD_DIM), BF16),
        compiler_params=_params(("arbitrary", "arbitrary")),
        name="project_heads",
    )(x, w)


def _outproj_kernel(*refs, n_x):
    x_refs, (w_ref, h_ref, g_ref, o_ref) = refs[:n_x], refs[n_x:]
    acc = None
    off = 0
    for x_ref in x_refs:
        kk = x_ref.shape[1]
        part = jnp.dot(x_ref[...], w_ref[off:off + kk, :], preferred_element_type=F32)
        acc = part if acc is None else acc + part
        off += kk
    o_ref[...] = h_ref[...] + g_ref[...] * acc


def out_project_residual(xs, w, h, mod, which_gate, row_of, tm, tn):
    m = h.shape[0]
    k = w.shape[0]
    in_specs = [pl.BlockSpec((tm, x.shape[1]), lambda j, i: (i, 0)) for x in xs]
    in_specs += [pl.BlockSpec((k, tn), lambda j, i: (0, j)),
                 pl.BlockSpec((tm, tn), lambda j, i: (i, j)),
                 pl.BlockSpec((None, None, 1, tn), lambda j, i: (row_of(i), which_gate, 0, j))]
    return pl.pallas_call(
        functools.partial(_outproj_kernel, n_x=len(xs)),
        grid=(D_MODEL // tn, m // tm),
        in_specs=in_specs,
        out_specs=pl.BlockSpec((tm, tn), lambda j, i: (i, j)),
        out_shape=jax.ShapeDtypeStruct((m, D_MODEL), F32),
        compiler_params=_params(("arbitrary", "arbitrary")),
        name="out_project_residual",
    )(*xs, w, h, mod)


def _mlp_kernel(h_ref, gn_ref, sh_ref, sc_ref, gate_ref, w1_ref, w2_ref, fg_ref, o_ref, xs_ref, hid_ref, *,
                final_norm):
    j = pl.program_id(1)

    @pl.when(j == 0)
    def _():
        _modulate_into(xs_ref, h_ref, gn_ref, sh_ref, sc_ref)
        o_ref[...] = jnp.zeros_like(o_ref)
        hid_ref[...] = jnp.zeros_like(hid_ref)

    o_ref[...] = jnp.dot(hid_ref[...], w2_ref[...], preferred_element_type=F32) + o_ref[...]
    hid = jnp.dot(xs_ref[...], w1_ref[...], preferred_element_type=F32)
    hid_ref[...] = jnp.square(jnp.maximum(hid, 0.0)).astype(BF16)

    @pl.when(j == pl.num_programs(1) - 1)
    def _():
        gate, fg = gate_ref[...], fg_ref[...]

        def rows_fn(rows):
            y = h_ref[rows, :] + gate * o_ref[rows, :]
            if final_norm:
                ms = jnp.mean(y * y, axis=-1, keepdims=True)
                y = y * lax.rsqrt(ms + EPS) * fg
            o_ref[rows, :] = y

        _row_chunks(h_ref.shape[0], rows_fn)


def mlp_residual(h, gn, mod, row_of, w1, w2, final_g, final_norm, tm, th):
    m = h.shape[0]
    nj = MLP_HIDDEN // th
    once = pl.Buffered(1)
    row = lambda i, j: (i, 0)
    return pl.pallas_call(
        functools.partial(_mlp_kernel, final_norm=final_norm),
        grid=(m // tm, nj + 1),
        in_specs=[pl.BlockSpec((tm, D_MODEL), row, pipeline_mode=once),
                  pl.BlockSpec((1, D_MODEL), lambda i, j: (0, 0)),
                  _mod_spec(3, row_of), _mod_spec(4, row_of), _mod_spec(5, row_of),
                  pl.BlockSpec((D_MODEL, th), lambda i, j: (0, jnp.minimum(j, nj - 1))),
                  pl.BlockSpec((th, D_MODEL), lambda i, j: (jnp.maximum(j - 1, 0), 0)),
                  pl.BlockSpec((1, D_MODEL), lambda i, j: (0, 0))],
        out_specs=pl.BlockSpec((tm, D_MODEL), row, pipeline_mode=once),
        out_shape=jax.ShapeDtypeStruct((m, D_MODEL), F32),
        scratch_shapes=[pltpu.VMEM((tm, D_MODEL), BF16), pltpu.VMEM((tm, th), BF16)],
        compiler_params=_params(("arbitrary", "arbitrary")),
        name="mlp_residual",
    )(h, gn.reshape(1, D_MODEL), mod, mod, mod, w1, w2, final_g.reshape(1, D_MODEL))


def _rotary_partner(lane):
    return jnp.where(lane % 64 < 32, lane + 32, lane - 32)


def _prep_kernel(x_ref, c_ref, s_ref, g_ref, pg_ref, perm_ref, o_ref):
    n = x_ref.shape[0]
    d = x_ref.shape[2]
    normed = pl.program_id(1) < N_NORMED // n
    c, s = c_ref[...], s_ref[...]
    xp = jnp.dot(jnp.concatenate([x_ref[i] for i in range(n)], axis=1), perm_ref[...], preferred_element_type=F32)
    for i in range(n):
        x = x_ref[i].astype(F32)
        ms = jnp.mean(x * x, axis=-1, keepdims=True)
        inv = jnp.where(normed, lax.rsqrt(ms + EPS), 1.0)
        o_ref[i] = ((x * (g_ref[i] * c) + xp[:, i * d:(i + 1) * d] * (pg_ref[i] * s)) * inv).astype(o_ref.dtype)


def prep_qk(proj, cos_t, sin_t, gains, tm):
    m = proj.shape[1]
    nt = cos_t.shape[0] // tm
    pair = 2
    assert N_NORMED % pair == 0 and N_ROT % pair == 0 and P_QB % pair == 0
    src = lambda j: jnp.where(j < N_NORMED // pair, j, j + (P_QB - N_NORMED) // pair)
    lane = jnp.arange(pair * HEAD_DIM, dtype=jnp.int32)
    partner = _rotary_partner(lane % HEAD_DIM) + (lane // HEAD_DIM) * HEAD_DIM
    perm = (lane[:, None] == partner[None, :]).astype(BF16)
    g = gains.reshape(N_ROT // pair, pair, 1, HEAD_DIM)
    pg = gains[..., partner[:HEAD_DIM]].reshape(N_ROT // pair, pair, 1, HEAD_DIM)
    vec = pl.BlockSpec((None, pair, 1, HEAD_DIM), lambda i, j: (j, 0, 0, 0))
    return pl.pallas_call(
        _prep_kernel,
        grid=(m // tm, N_ROT // pair),
        in_specs=[pl.BlockSpec((pair, tm, HEAD_DIM), lambda i, j: (src(j), i, 0)),
                  pl.BlockSpec((tm, HEAD_DIM), lambda i, j: (i % nt, 0)),
                  pl.BlockSpec((tm, HEAD_DIM), lambda i, j: (i % nt, 0)),
                  vec, vec,
                  pl.BlockSpec((pair * HEAD_DIM, pair * HEAD_DIM), lambda i, j: (0, 0))],
        out_specs=pl.BlockSpec((pair, tm, HEAD_DIM), lambda i, j: (j, i, 0)),
        out_shape=jax.ShapeDtypeStruct((N_ROT, m, HEAD_DIM), BF16),
        compiler_params=_params(("arbitrary", "arbitrary")),
        name="prep_qk",
    )(proj, cos_t, sin_t, g, pg, perm)


def _nt_dot(a, b):
    return lax.dot_general(a, b, (((1,), (1,)), ((), ())), preferred_element_type=F32)


def _rows(c, size):
    return pl.ds(c * size, size) if isinstance(c, int) else pl.ds(pl.multiple_of(c * size, size), size)


def _transpose_to(dst_ref, src):
    dst_ref[...] = src.astype(F32).T.astype(dst_ref.dtype)


def _pipeline3(n, stage1, stage2, stage3):
    if n == 1:
        stage3(0, stage2(0, stage1(0)))
        return
    x = {0: stage1(0), 1: stage1(1)}
    y = {0: stage2(0, x.pop(0))}
    for i in range(1, n):
        stage3(i - 1, y.pop(i - 1))
        if i + 1 < n:
            x[i + 1] = stage1(i + 1)
        y[i] = stage2(i, x.pop(i))
    stage3(n - 1, y.pop(n - 1))


class _OnlineSoftmax:
    def __init__(self, n_queries, dv):
        self.m = jnp.full((1, n_queries), NEG, F32)
        self.l = jnp.zeros((1, n_queries), F32)
        self.acc = jnp.zeros((dv, n_queries), F32)

    def probs(self, s):
        m_new = jnp.maximum(self.m, jnp.max(s, axis=0, keepdims=True))
        alpha = jnp.exp2(self.m - m_new)
        p = jnp.exp2(s - m_new)
        self.l = alpha * self.l + jnp.sum(p, axis=0, keepdims=True)
        self.m = m_new
        return p.astype(BF16), alpha

    def accumulate(self, vt, p_alpha):
        p, alpha = p_alpha
        self.acc = alpha * self.acc + jnp.dot(vt, p, preferred_element_type=F32)

    def result(self):
        return self.acc * (1.0 / self.l)


def _gqa_kernel(*refs, with_lat, tk):
    if with_lat:
        q_ref, kl_ref, vl_ref, kc_ref, vc_ref, o_ref, vlt_ref, vct_ref = refs
    else:
        q_ref, kc_ref, vc_ref, o_ref, vct_ref = refs
    g, tq, d = q_ref.shape
    n_lat = vlt_ref.shape[0] if with_lat else 0

    @pl.when(pl.program_id(2) == 0)
    def _():
        _transpose_to(vct_ref, vc_ref[...])
        if with_lat:
            def body(c, carry):
                _transpose_to(vlt_ref.at[c], vl_ref[_rows(c, tk), :])
                return carry
            lax.fori_loop(0, n_lat, body, 0)

    q = q_ref[...].reshape(g * tq, d)
    sm = _OnlineSoftmax(g * tq, d)
    keys = lambda c: kc_ref[...] if c == n_lat else kl_ref[_rows(c, tk), :]
    vals = lambda c: vct_ref[...] if c == n_lat else vlt_ref[c]
    _pipeline3(n_lat + 1,
               lambda c: _nt_dot(keys(c), q),
               lambda c, s: sm.probs(s),
               lambda c, pa: sm.accumulate(vals(c), pa))
    o = sm.result().T
    for h in range(g):
        o_ref[:, h * d:(h + 1) * d] = o[h * tq:(h + 1) * tq].astype(o_ref.dtype)


def gqa_attention(rot_q, proj_q, rot_lat, proj_lat, rot_ctx, proj_ctx, tq, tk):
    mq = rot_q.shape[1]
    nq = mq // BATCH // tq
    with_lat = rot_lat is not None
    q_spec = pl.BlockSpec((A_GROUP, tq, HEAD_DIM), lambda b, g, i: (g, b * nq + i, 0))
    kv = lambda base, n: pl.BlockSpec((None, n, HEAD_DIM), lambda b, g, i: (base + g, b, 0))
    in_specs, args = [q_spec], [rot_q]
    if with_lat:
        in_specs += [kv(R_KA, SEQ), kv(P_VA, SEQ)]
        args += [rot_lat, proj_lat]
    in_specs += [kv(R_KA, CTX_LEN), kv(P_VA, CTX_LEN)]
    args += [rot_ctx, proj_ctx]
    scratch = [pltpu.VMEM((SEQ // tk, HEAD_DIM, tk), BF16)] if with_lat else []
    scratch += [pltpu.VMEM((HEAD_DIM, CTX_LEN), BF16)]
    return pl.pallas_call(
        functools.partial(_gqa_kernel, with_lat=with_lat, tk=tk),
        grid=(BATCH, A_KV_HEADS, nq),
        in_specs=in_specs,
        out_specs=pl.BlockSpec((tq, A_GROUP * HEAD_DIM), lambda b, g, i: (b * nq + i, g)),
        out_shape=jax.ShapeDtypeStruct((mq, A_HEADS * HEAD_DIM), BF16),
        scratch_shapes=scratch,
        compiler_params=_params(("arbitrary", "arbitrary", "arbitrary")),
        name="gqa_attention",
    )(*args)


def _diff_kernel(*refs, with_lat, tk, lambda_init):
    if with_lat:
        (q_ref, kl_ref, vl_ref, kc_ref, vc_ref, lq1_ref, lk1_ref, lq2_ref, lk2_ref, sg_ref,
         o_ref, vlt_ref, vct_ref) = refs
    else:
        q_ref, kc_ref, vc_ref, lq1_ref, lk1_ref, lq2_ref, lk2_ref, sg_ref, o_ref, vct_ref = refs
    tq, d = q_ref.shape[1:]
    n_lat = vlt_ref.shape[0] if with_lat else 0

    @pl.when(pl.program_id(2) == 0)
    def _():
        for half in range(2):
            _transpose_to(vct_ref.at[half * d:(half + 1) * d, :], vc_ref[half])
        if with_lat:
            def body(c, carry):
                for half in range(2):
                    _transpose_to(vlt_ref.at[c, half * d:(half + 1) * d, :], vl_ref[half, _rows(c, tk), :])
                return carry
            lax.fori_loop(0, n_lat, body, 0)

    maps = (0, 1)
    sm = [_OnlineSoftmax(tq, 2 * d) for _ in maps]
    keys = lambda c, i: kc_ref[i] if c == n_lat else kl_ref[i, _rows(c, tk), :]
    vals = lambda c: vct_ref[...] if c == n_lat else vlt_ref[c]

    def accumulate(c, pas):
        vt = vals(c)
        for i in maps:
            sm[i].accumulate(vt, pas[i])

    _pipeline3(n_lat + 1,
               lambda c: [_nt_dot(keys(c, i), q_ref[i]) for i in maps],
               lambda c, ss: [sm[i].probs(ss[i]) for i in maps],
               accumulate)

    lam = (jnp.exp(jnp.sum(lq1_ref[...] * lk1_ref[...], axis=1, keepdims=True))
           - jnp.exp(jnp.sum(lq2_ref[...] * lk2_ref[...], axis=1, keepdims=True)) + lambda_init)
    o = (sm[0].result() - lam * sm[1].result()).T
    ms = jnp.mean(o * o, axis=-1, keepdims=True)
    o_ref[...] = ((o * lax.rsqrt(ms + EPS) * sg_ref[...]) * (1.0 - lambda_init)).astype(o_ref.dtype)


def diff_attention(rot_q, rot_lat, proj_lat, rot_ctx, proj_ctx, lam_params, subln_g, lambda_init, tq, tk):
    mq = rot_q.shape[1]
    nq = mq // BATCH // tq
    with_lat = rot_lat is not None
    pair = lambda base, n: pl.BlockSpec((2, n, HEAD_DIM), lambda b, h, i: (base // 2 + h, b, 0))
    in_specs = [pl.BlockSpec((2, tq, HEAD_DIM), lambda b, h, i: (R_QB // 2 + h, b * nq + i, 0))]
    args = [rot_q]
    if with_lat:
        in_specs += [pair(R_KB, SEQ), pair(P_VB, SEQ)]
        args += [rot_lat, proj_lat]
    in_specs += [pair(R_KB, CTX_LEN), pair(P_VB, CTX_LEN)]
    args += [rot_ctx, proj_ctx]
    vec = lambda n: pl.BlockSpec((1, n), lambda b, h, i: (0, 0))
    in_specs += [vec(HEAD_DIM)] * 4 + [vec(2 * HEAD_DIM)]
    args += [p.reshape(1, HEAD_DIM) for p in lam_params] + [subln_g.reshape(1, 2 * HEAD_DIM)]
    scratch = [pltpu.VMEM((SEQ // tk, 2 * HEAD_DIM, tk), BF16)] if with_lat else []
    scratch += [pltpu.VMEM((2 * HEAD_DIM, CTX_LEN), BF16)]
    return pl.pallas_call(
        functools.partial(_diff_kernel, with_lat=with_lat, tk=tk, lambda_init=lambda_init),
        grid=(BATCH, B_HEADS, nq),
        in_specs=in_specs,
        out_specs=pl.BlockSpec((tq, 2 * HEAD_DIM), lambda b, h, i: (b * nq + i, h)),
        out_shape=jax.ShapeDtypeStruct((mq, B_HEADS * 2 * HEAD_DIM), BF16),
        scratch_shapes=scratch,
        compiler_params=_params(("arbitrary", "arbitrary", "arbitrary")),
        name="diff_attention",
    )(*args)


NA_QROWS = 4
NA_CHUNK = NA_QROWS * GRID_W
NA_BAND_CHUNKS = 3
N_DR = 2 * NA_ROWS - 1
NA_ONLY_A, NA_ONLY_B, NA_NEITHER = N_DR - 1, 2 * N_DR - 1, 3 * N_DR - 1


def _na_tile_index(key_row, row_a):
    first = lambda r: min(max(r - NA_ROWS // 2, 0), ROWS - NA_ROWS)
    va, vb = (first(r) <= key_row < first(r) + NA_ROWS for r in (row_a, row_a + 1))
    da = key_row - row_a + (NA_ROWS - 1)
    return da - 1 if (va and vb) else NA_ONLY_A + da if va else NA_ONLY_B - 1 + da if vb else NA_NEITHER


def _na_kernel(q_ref, k_ref, v_ref, kc_ref, vc_ref, tab_ref, o_ref, vt_ref, vct_ref):
    n_chunks = vt_ref.shape[0]
    chunk = lambda c: _rows(c, NA_CHUNK)
    band_start = lambda blk: min(max(blk - 1, 0), n_chunks - NA_BAND_CHUNKS)

    _transpose_to(vct_ref, vc_ref[...])

    def transpose_chunk(c, carry):
        _transpose_to(vt_ref.at[c], v_ref[chunk(c), :])
        return carry
    lax.fori_loop(0, n_chunks, transpose_chunk, 0)

    def scores(blk):
        row0 = blk * NA_QROWS
        cb = band_start(blk)
        q = q_ref[chunk(blk), :]
        s = []
        for c in range(NA_BAND_CHUNKS):
            tiles = []
            for i in range(NA_QROWS):
                key_row = (cb + c) * NA_QROWS + i
                tiles.append(jnp.concatenate(
                    [tab_ref[_na_tile_index(key_row, row0 + 2 * j)] for j in range(NA_QROWS // 2)], axis=1))
            s.append(_nt_dot(k_ref[chunk(cb + c), :], q) * SCALE_LOG2 + jnp.concatenate(tiles, axis=0))
        s.append(_nt_dot(kc_ref[...], q) * SCALE_LOG2)
        return s

    def probs(blk, s):
        m = functools.reduce(jnp.maximum, [jnp.max(x, axis=0, keepdims=True) for x in s])
        p = [jnp.exp2(x - m) for x in s]
        l = functools.reduce(jnp.add, [jnp.sum(x, axis=0, keepdims=True) for x in p])
        return [x.astype(BF16) for x in p], 1.0 / l

    def output(blk, p_linv):
        p, linv = p_linv
        cb = band_start(blk)
        o_t = jnp.dot(vct_ref[...], p[NA_BAND_CHUNKS], preferred_element_type=F32)
        for c in range(NA_BAND_CHUNKS):
            o_t = o_t + jnp.dot(vt_ref[cb + c], p[c], preferred_element_type=F32)
        o_ref[chunk(blk), :] = (o_t * linv).T.astype(o_ref.dtype)

    _pipeline3(n_chunks, scores, probs, output)


def neighbourhood_attention(proj_lat, proj_ctx, bias_table):
    assert CTX_LEN == NA_CHUNK
    lat = lambda base: pl.BlockSpec((None, SEQ, HEAD_DIM), lambda b, h: (base + h, b, 0))
    ctx = lambda base: pl.BlockSpec((None, CTX_LEN, HEAD_DIM), lambda b, h: (base + h, b, 0))
    return pl.pallas_call(
        _na_kernel,
        grid=(BATCH, C_HEADS),
        in_specs=[lat(0), lat(C_HEADS), lat(2 * C_HEADS), ctx(0), ctx(C_HEADS),
                  pl.BlockSpec((None, 3 * N_DR, GRID_W, 2 * GRID_W), lambda b, h: (h, 0, 0, 0))],
        out_specs=pl.BlockSpec((SEQ, HEAD_DIM), lambda b, h: (b, h)),
        out_shape=jax.ShapeDtypeStruct((BATCH * SEQ, C_HEADS * HEAD_DIM), BF16),
        scratch_shapes=[pltpu.VMEM((SEQ // NA_CHUNK, HEAD_DIM, NA_CHUNK), BF16),
                        pltpu.VMEM((HEAD_DIM, CTX_LEN), BF16)],
        compiler_params=_params(("arbitrary", "arbitrary")),
        name="neighbourhood_attention",
    )(proj_lat, proj_lat, proj_lat, proj_ctx, proj_ctx, bias_table)


def _rope_tables():
    t = jnp.arange(SEQ, dtype=jnp.int32)
    row = (t // GRID_W).astype(F32)
    col = (t % GRID_W).astype(F32)
    n_freq = HEAD_DIM // 4
    freqs = ROPE_THETA ** (-jnp.arange(n_freq, dtype=F32) / n_freq)
    ar, ac = row[:, None] * freqs, col[:, None] * freqs
    cos_t = jnp.concatenate([jnp.cos(ar), jnp.cos(ar), jnp.cos(ac), jnp.cos(ac)], axis=1)
    sin_t = jnp.concatenate([-jnp.sin(ar), jnp.sin(ar), -jnp.sin(ac), jnp.sin(ac)], axis=1)
    return cos_t, sin_t


def _na_bias_table(rel_bias):
    col = jnp.arange(GRID_W, dtype=jnp.int32)
    c0 = jnp.clip(col - NA_COLS // 2, 0, GRID_W - NA_COLS)
    key, qry = col[:, None], col[None, :]
    dc = key - qry + (NA_COLS - 1)
    valid = (key >= c0[None, :]) & (key < c0[None, :] + NA_COLS)
    t = rel_bias[:, :, jnp.clip(dc, 0, 2 * NA_COLS - 2)].astype(F32) * LOG2E
    t = jnp.where(valid[None, None], t, NEG)
    off = jnp.full_like(t, NEG)
    both = jnp.concatenate([t[:, 1:], t[:, :-1]], axis=-1)
    only_a = jnp.concatenate([t, off], axis=-1)
    only_b = jnp.concatenate([off, t], axis=-1)
    neither = jnp.concatenate([off[:, :1], off[:, :1]], axis=-1)
    return jnp.concatenate([both, only_a, only_b, neither], axis=1)


def kernel(x, c, ctx, c_ctx, ada_w, ada_b, norm1_g, norm2_g, w_in_even, w_out_even, a_q_norm, a_k_norm,
           b_lambda_q1, b_lambda_k1, b_lambda_q2, b_lambda_k2, b_subln_g, w_in_odd, w_out_odd, na_rel_bias,
           mlp_w1, mlp_w2, final_g):
    TM = 1024
    TC = BATCH * CTX_LEN
    TN = 512
    lat_row = lambda m: m // (SEQ // TM)
    lat_row_n = lambda m: m // (SEQ // TN)
    ctx_row = lambda m: 2

    craw = jnp.concatenate([c, c_ctx[None], jnp.zeros((5, D_MODEL), F32)], axis=0)
    mod = ada_modulation(craw, ada_w, ada_b).reshape(DEPTH, 8, 6, 1, D_MODEL)

    h = x.reshape(BATCH * SEQ, D_MODEL)
    hc = ctx.reshape(TC, D_MODEL)
    cos_t, sin_t = _rope_tables()
    one_t, zero_t = jnp.ones((TC, HEAD_DIM), F32), jnp.zeros((TC, HEAD_DIM), F32)

    mod0 = mod[0]
    w_in = cast_layer_bf16(w_in_even, 0)
    w_out = cast_layer_bf16(w_out_even, 0)
    ones = jnp.ones((HEAD_DIM,), F32)
    gains = jnp.stack([a_q_norm[0] * SCALE_LOG2] * A_HEADS + [a_k_norm[0]] * A_KV_HEADS
                      + [ones * SCALE_LOG2] * (2 * B_HEADS) + [ones] * (2 * B_HEADS)).reshape(N_ROT, 1, HEAD_DIM)
    lam_params = (b_lambda_q1[0], b_lambda_k1[0], b_lambda_q2[0], b_lambda_k2[0])
    lambda_init = 0.8 - 0.6 * math.exp(-0.3 * 0)

    u = norm_modulate(h, norm1_g[0], mod0, 0, 1, lat_row_n, TN)
    uc = norm_modulate(hc, norm1_g[0], mod0, 0, 1, ctx_row, TC)
    proj = project_heads(u, w_in, TM, 1024)
    proj_c = project_heads(uc, w_in, TC, 1024)
    rot = prep_qk(proj, cos_t, sin_t, gains, 2048)
    rot_c = prep_qk(proj_c, one_t, zero_t, gains, TC)

    a_lat = gqa_attention(rot, proj, rot, proj, rot_c, proj_c, 256, 1024)
    b_lat = diff_attention(rot, rot, proj, rot_c, proj_c, lam_params, b_subln_g[0], lambda_init, 512, 1024)
    a_ctx = gqa_attention(rot_c, proj_c, None, None, rot_c, proj_c, 128, 512)
    b_ctx = diff_attention(rot_c, None, None, rot_c, proj_c, lam_params, b_subln_g[0], lambda_init, 256, 512)

    h = out_project_residual([a_lat, b_lat], w_out, h, mod0, 2, lat_row, TM, 1024)
    hc = out_project_residual([a_ctx, b_ctx], w_out, hc, mod0, 2, ctx_row, TC, 1024)
    w1 = cast_layer_bf16(mlp_w1, 0)
    w2 = cast_layer_bf16(mlp_w2, 0)
    h = mlp_residual(h, norm2_g[0], mod0, lat_row, w1, w2, final_g, False, TM, 512)
    hc = mlp_residual(hc, norm2_g[0], mod0, ctx_row, w1, w2, final_g, False, TC, 512)

    mod1 = mod[1]
    w_in = cast_layer_bf16(w_in_odd, 0)
    u = norm_modulate(h, norm1_g[1], mod1, 0, 1, lat_row_n, TN)
    uc = norm_modulate(hc, norm1_g[1], mod1, 0, 1, ctx_row, TC)
    proj = project_heads(u, w_in, TM, 1024)
    proj_c = project_heads(uc, w_in, TC, 1024, col0=C_HEADS * HEAD_DIM)
    attn = neighbourhood_attention(proj, proj_c, _na_bias_table(na_rel_bias[0]))
    h = out_project_residual([attn], cast_layer_bf16(w_out_odd, 0), h, mod1, 2, lat_row, TM, 1024)
    out = mlp_residual(h, norm2_g[1], mod1, lat_row, cast_layer_bf16(mlp_w1, 1), cast_layer_bf16(mlp_w2, 1),
                       final_g, True, TM, 512)
    return out.reshape(BATCH, SEQ, D_MODEL)
```

```python
import functools
import math

import jax
import jax.numpy as jnp
from jax import lax
from jax.experimental import pallas as pl
from jax.experimental.pallas import tpu as pltpu

D_MODEL = 4096
BATCH = 2
SEQ = 8192
DEPTH = 2
GRID_W = 64
ROWS = SEQ // GRID_W
CTX_LEN = 256
HEAD_DIM = 128
ROPE_THETA = 10000.0
EPS = 1e-6
A_HEADS = 16
A_KV_HEADS = 4
A_GROUP = A_HEADS // A_KV_HEADS
B_HEADS = 8
C_HEADS = 32
NA_ROWS = 8
NA_COLS = 16
MLP_HIDDEN = 4 * D_MODEL
EVEN_IN = 9216
SCALE = HEAD_DIM ** -0.5
LOG2E = 1.4426950408889634
SCALE_LOG2 = SCALE * LOG2E

P_QA, P_KA, P_VA, P_QB, P_KB, P_VB = 0, 16, 20, 24, 40, 56
N_PROJ_EVEN = EVEN_IN // HEAD_DIM
R_QA, R_KA, R_QB, R_KB = 0, 16, 20, 36
N_ROT = 52
N_NORMED = 20

VMEM_LIMIT = 62 * 1024 * 1024
NEG = -1e30

F32 = jnp.float32
BF16 = jnp.bfloat16


def _params(sem, vmem=VMEM_LIMIT):
    return pltpu.CompilerParams(dimension_semantics=sem, vmem_limit_bytes=vmem)


def _ada_kernel(c_ref, w_ref, b_ref, o_ref):
    x = c_ref[...]
    cond = x / (1.0 + jnp.exp(-x))
    o_ref[...] = jnp.dot(cond.astype(BF16), w_ref[...].astype(BF16),
                         preferred_element_type=F32) + b_ref[...]


def ada_modulation(craw, ada_w, ada_b):
    tn = 512
    n6 = 6 * D_MODEL
    return pl.pallas_call(
        _ada_kernel,
        grid=(DEPTH, n6 // tn),
        in_specs=[pl.BlockSpec((8, D_MODEL), lambda i, n: (0, 0)),
                  pl.BlockSpec((None, D_MODEL, tn), lambda i, n: (i, 0, n)),
                  pl.BlockSpec((None, 1, tn), lambda i, n: (i, 0, n))],
        out_specs=pl.BlockSpec((None, 8, tn), lambda i, n: (i, 0, n)),
        out_shape=jax.ShapeDtypeStruct((DEPTH, 8, n6), F32),
        compiler_params=_params(("arbitrary", "arbitrary")),
        name="ada_modulation",
    )(craw, ada_w, ada_b.reshape(DEPTH, 1, n6))


ROW_CHUNK = 16
ROW_UNROLL = 4


def _row_chunks(n_rows, fn):
    def body(i, carry):
        fn(pl.ds(pl.multiple_of(i * ROW_CHUNK, ROW_CHUNK), ROW_CHUNK))
        return carry
    lax.fori_loop(0, n_rows // ROW_CHUNK, body, 0, unroll=ROW_UNROLL)


def _modulate_into(dst_ref, h_ref, g_ref, sh_ref, sc_ref):
    g, up, sh = g_ref[...], 1.0 + sc_ref[...], sh_ref[...]

    def rows_fn(rows):
        x = h_ref[rows, :]
        ms = jnp.mean(x * x, axis=-1, keepdims=True)
        dst_ref[rows, :] = ((x * lax.rsqrt(ms + EPS) * g) * up + sh).astype(dst_ref.dtype)

    _row_chunks(h_ref.shape[0], rows_fn)


def _normmod_kernel(h_ref, g_ref, sh_ref, sc_ref, o_ref):
    _modulate_into(o_ref, h_ref, g_ref, sh_ref, sc_ref)


def _mod_spec(which, row_of):
    return pl.BlockSpec((None, None, 1, D_MODEL), lambda m, *_: (row_of(m), which, 0, 0))


def norm_modulate(h, g, mod, which_shift, which_scale, row_of, tm):
    m = h.shape[0]
    return pl.pallas_call(
        _normmod_kernel,
        grid=(m // tm,),
        in_specs=[pl.BlockSpec((tm, D_MODEL), lambda i: (i, 0)),
                  pl.BlockSpec((1, D_MODEL), lambda i: (0, 0)),
                  _mod_spec(which_shift, row_of), _mod_spec(which_scale, row_of)],
        out_specs=pl.BlockSpec((tm, D_MODEL), lambda i: (i, 0)),
        out_shape=jax.ShapeDtypeStruct((m, D_MODEL), BF16),
        compiler_params=_params(("arbitrary",)),
        name="norm_modulate",
    )(h, g.reshape(1, D_MODEL), mod, mod)


def _proj_kernel(x_ref, w_ref, o_ref):
    acc = jnp.dot(x_ref[...], w_ref[...], preferred_element_type=F32)
    for j in range(o_ref.shape[0]):
        o_ref[j] = acc[:, j * HEAD_DIM:(j + 1) * HEAD_DIM].astype(o_ref.dtype)


def _cast_kernel(x_ref, o_ref):
    o_ref[...] = x_ref[...].astype(o_ref.dtype)


CAST_BLOCK_BYTES = 8 * 1024 * 1024


def cast_layer_bf16(w, layer):
    _, r, c = w.shape
    rows = 1 << int(math.log2(CAST_BLOCK_BYTES // (4 * c)))
    assert r % rows == 0
    return pl.pallas_call(
        _cast_kernel,
        grid=(r // rows,),
        in_specs=[pl.BlockSpec((None, rows, c), lambda i: (layer, i, 0))],
        out_specs=pl.BlockSpec((rows, c), lambda i: (i, 0)),
        out_shape=jax.ShapeDtypeStruct((r, c), BF16),
        compiler_params=_params(("arbitrary",)),
        name="cast_layer_bf16",
    )(w)


def project_heads(x, w, tm, tn, col0=0):
    m, k = x.shape
    n = w.shape[1] - col0
    assert col0 % tn == 0
    return pl.pallas_call(
        _proj_kernel,
        grid=(n // tn, m // tm),
        in_specs=[pl.BlockSpec((tm, k), lambda j, i: (i, 0)),
                  pl.BlockSpec((k, tn), lambda j, i: (0, j + col0 // tn))],
        out_specs=pl.BlockSpec((tn // HEAD_DIM, tm, HEAD_DIM), lambda j, i: (j, i, 0)),
        out_shape=jax.ShapeDtypeStruct((n // HEAD_DIM, m, HEAD_DIM), BF16),
        compiler_params=_params(("arbitrary", "arbitrary")),
        name="project_heads",
    )(x, w)


def _outproj_kernel(*refs, n_x):
    x_refs, (w_ref, h_ref, g_ref, o_ref) = refs[:n_x], refs[n_x:]
    acc = None
    off = 0
    for x_ref in x_refs:
        kk = x_ref.shape[1]
        part = jnp.dot(x_ref[...], w_ref[off:off + kk, :], preferred_element_type=F32)
        acc = part if acc is None else acc + part
        off += kk
    o_ref[...] = h_ref[...] + g_ref[...] * acc


def out_project_residual(xs, w, h, mod, which_gate, row_of, tm, tn):
    m = h.shape[0]
    k = w.shape[0]
    in_specs = [pl.BlockSpec((tm, x.shape[1]), lambda j, i: (i, 0)) for x in xs]
    in_specs += [pl.BlockSpec((k, tn), lambda j, i: (0, j)),
                 pl.BlockSpec((tm, tn), lambda j, i: (i, j)),
                 pl.BlockSpec((None, None, 1, tn), lambda j, i: (row_of(i), which_gate, 0, j))]
    return pl.pallas_call(
        functools.partial(_outproj_kernel, n_x=len(xs)),
        grid=(D_MODEL // tn, m // tm),
        in_specs=in_specs,
        out_specs=pl.BlockSpec((tm, tn), lambda j, i: (i, j)),
        out_shape=jax.ShapeDtypeStruct((m, D_MODEL), F32),
        compiler_params=_params(("arbitrary", "arbitrary")),
        name="out_project_residual",
    )(*xs, w, h, mod)


def _mlp_kernel(h_ref, gn_ref, sh_ref, sc_ref, gate_ref, w1_ref, w2_ref, fg_ref, o_ref, xs_ref, hid_ref, *,
                final_norm, nj, n_pairs):
    t = pl.program_id(0)
    prev = jnp.maximum(t - 1, 0)

    @pl.when(jnp.logical_and(t % nj == 0, t < n_pairs))
    def _():
        _modulate_into(xs_ref, h_ref, gn_ref, sh_ref, sc_ref)

    @pl.when(t == 0)
    def _():
        hid_ref[...] = jnp.zeros_like(hid_ref)
        o_ref[...] = jnp.zeros_like(o_ref)

    @pl.when(jnp.logical_and(t > 0, prev % nj == 0))
    def _():
        def copy_rows(rows):
            o_ref[rows, :] = h_ref[rows, :]
        _row_chunks(h_ref.shape[0], copy_rows)

    o_ref[...] = gate_ref[...] * jnp.dot(hid_ref[...], w2_ref[...], preferred_element_type=F32) + o_ref[...]
    hid = jnp.dot(xs_ref[...], w1_ref[...], preferred_element_type=F32)
    hid_ref[...] = jnp.square(jnp.maximum(hid, 0.0)).astype(BF16)

    if final_norm:
        @pl.when(jnp.logical_and(t > 0, prev % nj == nj - 1))
        def _():
            fg = fg_ref[...]

            def norm_rows(rows):
                y = o_ref[rows, :]
                ms = jnp.mean(y * y, axis=-1, keepdims=True)
                o_ref[rows, :] = y * lax.rsqrt(ms + EPS) * fg

            _row_chunks(o_ref.shape[0], norm_rows)


def mlp_residual(h, gn, mod, row_of, w1, w2, final_g, final_norm, tm, th):
    m = h.shape[0]
    nj = MLP_HIDDEN // th
    n_tiles = m // tm
    n_pairs = n_tiles * nj
    once = pl.Buffered(1)
    tile_cur = lambda t: jnp.minimum(t // nj, n_tiles - 1)
    tile_prev = lambda t: jnp.maximum(t - 1, 0) // nj
    mod_row = lambda which, tile: pl.BlockSpec((None, None, 1, D_MODEL),
                                               lambda t: (row_of(tile(t)), which, 0, 0))
    return pl.pallas_call(
        functools.partial(_mlp_kernel, final_norm=final_norm, nj=nj, n_pairs=n_pairs),
        grid=(n_pairs + 1,),
        in_specs=[pl.BlockSpec((tm, D_MODEL), lambda t: (tile_cur(t), 0), pipeline_mode=once),
                  pl.BlockSpec((1, D_MODEL), lambda t: (0, 0)),
                  mod_row(3, tile_cur), mod_row(4, tile_cur), mod_row(5, tile_prev),
                  pl.BlockSpec((D_MODEL, th), lambda t: (0, t % nj)),
                  pl.BlockSpec((th, D_MODEL), lambda t: (jnp.maximum(t - 1, 0) % nj, 0)),
                  pl.BlockSpec((1, D_MODEL), lambda t: (0, 0))],
        out_specs=pl.BlockSpec((tm, D_MODEL), lambda t: (tile_prev(t), 0), pipeline_mode=once),
        out_shape=jax.ShapeDtypeStruct((m, D_MODEL), F32),
        scratch_shapes=[pltpu.VMEM((tm, D_MODEL), BF16), pltpu.VMEM((tm, th), BF16)],
        compiler_params=_params(("arbitrary",)),
        name="mlp_residual",
    )(h, gn.reshape(1, D_MODEL), mod, mod, mod, w1, w2, final_g.reshape(1, D_MODEL))


def _rotary_partner(lane):
    return jnp.where(lane % 64 < 32, lane + 32, lane - 32)


def _prep_kernel(x_ref, c_ref, s_ref, g_ref, pg_ref, perm_ref, o_ref):
    n = x_ref.shape[0]
    d = x_ref.shape[2]
    normed = pl.program_id(1) < N_NORMED // n
    c, s = c_ref[...], s_ref[...]
    xp = jnp.dot(jnp.concatenate([x_ref[i] for i in range(n)], axis=1), perm_ref[...], preferred_element_type=F32)
    for i in range(n):
        x = x_ref[i].astype(F32)
        ms = jnp.mean(x * x, axis=-1, keepdims=True)
        inv = jnp.where(normed, lax.rsqrt(ms + EPS), 1.0)
        o_ref[i] = ((x * (g_ref[i] * c) + xp[:, i * d:(i + 1) * d] * (pg_ref[i] * s)) * inv).astype(o_ref.dtype)


def prep_qk(proj, cos_t, sin_t, gains, tm):
    m = proj.shape[1]
    nt = cos_t.shape[0] // tm
    pair = 2
    assert N_NORMED % pair == 0 and N_ROT % pair == 0 and P_QB % pair == 0
    src = lambda j: jnp.where(j < N_NORMED // pair, j, j + (P_QB - N_NORMED) // pair)
    lane = jnp.arange(pair * HEAD_DIM, dtype=jnp.int32)
    partner = _rotary_partner(lane % HEAD_DIM) + (lane // HEAD_DIM) * HEAD_DIM
    perm = (lane[:, None] == partner[None, :]).astype(BF16)
    g = gains.reshape(N_ROT // pair, pair, 1, HEAD_DIM)
    pg = gains[..., partner[:HEAD_DIM]].reshape(N_ROT // pair, pair, 1, HEAD_DIM)
    vec = pl.BlockSpec((None, pair, 1, HEAD_DIM), lambda i, j: (j, 0, 0, 0))
    return pl.pallas_call(
        _prep_kernel,
        grid=(m // tm, N_ROT // pair),
        in_specs=[pl.BlockSpec((pair, tm, HEAD_DIM), lambda i, j: (src(j), i, 0)),
                  pl.BlockSpec((tm, HEAD_DIM), lambda i, j: (i % nt, 0)),
                  pl.BlockSpec((tm, HEAD_DIM), lambda i, j: (i % nt, 0)),
                  vec, vec,
                  pl.BlockSpec((pair * HEAD_DIM, pair * HEAD_DIM), lambda i, j: (0, 0))],
        out_specs=pl.BlockSpec((pair, tm, HEAD_DIM), lambda i, j: (j, i, 0)),
        out_shape=jax.ShapeDtypeStruct((N_ROT, m, HEAD_DIM), BF16),
        compiler_params=_params(("arbitrary", "arbitrary")),
        name="prep_qk",
    )(proj, cos_t, sin_t, g, pg, perm)


def _nt_dot(a, b):
    return lax.dot_general(a, b, (((1,), (1,)), ((), ())), preferred_element_type=F32)


def _rows(c, size):
    return pl.ds(c * size, size) if isinstance(c, int) else pl.ds(pl.multiple_of(c * size, size), size)


def _transpose_to(dst_ref, src):
    dst_ref[...] = src.astype(F32).T.astype(dst_ref.dtype)


def _pipeline3(n, stage1, stage2, stage3):
    if n == 1:
        stage3(0, stage2(0, stage1(0)))
        return
    x = {0: stage1(0), 1: stage1(1)}
    y = {0: stage2(0, x.pop(0))}
    for i in range(1, n):
        stage3(i - 1, y.pop(i - 1))
        if i + 1 < n:
            x[i + 1] = stage1(i + 1)
        y[i] = stage2(i, x.pop(i))
    stage3(n - 1, y.pop(n - 1))


class _OnlineSoftmax:
    def __init__(self, n_queries, dv):
        self.m = jnp.full((1, n_queries), NEG, F32)
        self.l = jnp.zeros((1, n_queries), F32)
        self.acc = jnp.zeros((dv, n_queries), F32)

    def probs(self, s):
        m_new = jnp.maximum(self.m, jnp.max(s, axis=0, keepdims=True))
        alpha = jnp.exp2(self.m - m_new)
        p = jnp.exp2(s - m_new)
        self.l = alpha * self.l + jnp.sum(p, axis=0, keepdims=True)
        self.m = m_new
        return p.astype(BF16), alpha

    def accumulate(self, vt, p_alpha):
        p, alpha = p_alpha
        self.acc = alpha * self.acc + jnp.dot(vt, p, preferred_element_type=F32)

    def result(self):
        return self.acc * (1.0 / self.l)


def _gqa_kernel(*refs, with_lat, tk):
    if with_lat:
        q_ref, kl_ref, vl_ref, kc_ref, vc_ref, o_ref, vlt_ref, vct_ref = refs
    else:
        q_ref, kc_ref, vc_ref, o_ref, vct_ref = refs
    g, tq, d = q_ref.shape
    n_lat = vlt_ref.shape[0] if with_lat else 0

    @pl.when(pl.program_id(2) == 0)
    def _():
        _transpose_to(vct_ref, vc_ref[...])
        if with_lat:
            def body(c, carry):
                _transpose_to(vlt_ref.at[c], vl_ref[_rows(c, tk), :])
                return carry
            lax.fori_loop(0, n_lat, body, 0)

    q = q_ref[...].reshape(g * tq, d)
    sm = _OnlineSoftmax(g * tq, d)
    keys = lambda c: kc_ref[...] if c == n_lat else kl_ref[_rows(c, tk), :]
    vals = lambda c: vct_ref[...] if c == n_lat else vlt_ref[c]
    _pipeline3(n_lat + 1,
               lambda c: _nt_dot(keys(c), q),
               lambda c, s: sm.probs(s),
               lambda c, pa: sm.accumulate(vals(c), pa))
    o = sm.result().T
    for h in range(g):
        o_ref[:, h * d:(h + 1) * d] = o[h * tq:(h + 1) * tq].astype(o_ref.dtype)


def gqa_attention(rot_q, proj_q, rot_lat, proj_lat, rot_ctx, proj_ctx, tq, tk):
    mq = rot_q.shape[1]
    nq = mq // BATCH // tq
    with_lat = rot_lat is not None
    q_spec = pl.BlockSpec((A_GROUP, tq, HEAD_DIM), lambda b, g, i: (g, b * nq + i, 0))
    kv = lambda base, n: pl.BlockSpec((None, n, HEAD_DIM), lambda b, g, i: (base + g, b, 0))
    in_specs, args = [q_spec], [rot_q]
    if with_lat:
        in_specs += [kv(R_KA, SEQ), kv(P_VA, SEQ)]
        args += [rot_lat, proj_lat]
    in_specs += [kv(R_KA, CTX_LEN), kv(P_VA, CTX_LEN)]
    args += [rot_ctx, proj_ctx]
    scratch = [pltpu.VMEM((SEQ // tk, HEAD_DIM, tk), BF16)] if with_lat else []
    scratch += [pltpu.VMEM((HEAD_DIM, CTX_LEN), BF16)]
    return pl.pallas_call(
        functools.partial(_gqa_kernel, with_lat=with_lat, tk=tk),
        grid=(BATCH, A_KV_HEADS, nq),
        in_specs=in_specs,
        out_specs=pl.BlockSpec((tq, A_GROUP * HEAD_DIM), lambda b, g, i: (b * nq + i, g)),
        out_shape=jax.ShapeDtypeStruct((mq, A_HEADS * HEAD_DIM), BF16),
        scratch_shapes=scratch,
        compiler_params=_params(("arbitrary", "arbitrary", "arbitrary")),
        name="gqa_attention",
    )(*args)


def _diff_kernel(*refs, with_lat, tk, lambda_init):
    if with_lat:
        (q_ref, kl_ref, vl_ref, kc_ref, vc_ref, lq1_ref, lk1_ref, lq2_ref, lk2_ref, sg_ref,
         o_ref, vlt_ref, vct_ref) = refs
    else:
        q_ref, kc_ref, vc_ref, lq1_ref, lk1_ref, lq2_ref, lk2_ref, sg_ref, o_ref, vct_ref = refs
    tq, d = q_ref.shape[1:]
    n_lat = vlt_ref.shape[0] if with_lat else 0

    @pl.when(pl.program_id(2) == 0)
    def _():
        for half in range(2):
            _transpose_to(vct_ref.at[half * d:(half + 1) * d, :], vc_ref[half])
        if with_lat:
            def body(c, carry):
                for half in range(2):
                    _transpose_to(vlt_ref.at[c, half * d:(half + 1) * d, :], vl_ref[half, _rows(c, tk), :])
                return carry
            lax.fori_loop(0, n_lat, body, 0)

    maps = (0, 1)
    sm = [_OnlineSoftmax(tq, 2 * d) for _ in maps]
    keys = lambda c, i: kc_ref[i] if c == n_lat else kl_ref[i, _rows(c, tk), :]
    vals = lambda c: vct_ref[...] if c == n_lat else vlt_ref[c]

    def accumulate(c, pas):
        vt = vals(c)
        for i in maps:
            sm[i].accumulate(vt, pas[i])

    _pipeline3(n_lat + 1,
               lambda c: [_nt_dot(keys(c, i), q_ref[i]) for i in maps],
               lambda c, ss: [sm[i].probs(ss[i]) for i in maps],
               accumulate)

    lam = (jnp.exp(jnp.sum(lq1_ref[...] * lk1_ref[...], axis=1, keepdims=True))
           - jnp.exp(jnp.sum(lq2_ref[...] * lk2_ref[...], axis=1, keepdims=True)) + lambda_init)
    o = (sm[0].result() - lam * sm[1].result()).T
    ms = jnp.mean(o * o, axis=-1, keepdims=True)
    o_ref[...] = ((o * lax.rsqrt(ms + EPS) * sg_ref[...]) * (1.0 - lambda_init)).astype(o_ref.dtype)


def diff_attention(rot_q, rot_lat, proj_lat, rot_ctx, proj_ctx, lam_params, subln_g, lambda_init, tq, tk):
    mq = rot_q.shape[1]
    nq = mq // BATCH // tq
    with_lat = rot_lat is not None
    pair = lambda base, n: pl.BlockSpec((2, n, HEAD_DIM), lambda b, h, i: (base // 2 + h, b, 0))
    in_specs = [pl.BlockSpec((2, tq, HEAD_DIM), lambda b, h, i: (R_QB // 2 + h, b * nq + i, 0))]
    args = [rot_q]
    if with_lat:
        in_specs += [pair(R_KB, SEQ), pair(P_VB, SEQ)]
        args += [rot_lat, proj_lat]
    in_specs += [pair(R_KB, CTX_LEN), pair(P_VB, CTX_LEN)]
    args += [rot_ctx, proj_ctx]
    vec = lambda n: pl.BlockSpec((1, n), lambda b, h, i: (0, 0))
    in_specs += [vec(HEAD_DIM)] * 4 + [vec(2 * HEAD_DIM)]
    args += [p.reshape(1, HEAD_DIM) for p in lam_params] + [subln_g.reshape(1, 2 * HEAD_DIM)]
    scratch = [pltpu.VMEM((SEQ // tk, 2 * HEAD_DIM, tk), BF16)] if with_lat else []
    scratch += [pltpu.VMEM((2 * HEAD_DIM, CTX_LEN), BF16)]
    return pl.pallas_call(
        functools.partial(_diff_kernel, with_lat=with_lat, tk=tk, lambda_init=lambda_init),
        grid=(BATCH, B_HEADS, nq),
        in_specs=in_specs,
        out_specs=pl.BlockSpec((tq, 2 * HEAD_DIM), lambda b, h, i: (b * nq + i, h)),
        out_shape=jax.ShapeDtypeStruct((mq, B_HEADS * 2 * HEAD_DIM), BF16),
        scratch_shapes=scratch,
        compiler_params=_params(("arbitrary", "arbitrary", "arbitrary")),
        name="diff_attention",
    )(*args)


NA_QROWS = 4
NA_CHUNK = NA_QROWS * GRID_W
NA_BAND_CHUNKS = 3
N_DR = 2 * NA_ROWS - 1
NA_ONLY_A, NA_ONLY_B, NA_NEITHER = N_DR - 1, 2 * N_DR - 1, 3 * N_DR - 1


def _na_tile_index(key_row, row_a):
    first = lambda r: min(max(r - NA_ROWS // 2, 0), ROWS - NA_ROWS)
    va, vb = (first(r) <= key_row < first(r) + NA_ROWS for r in (row_a, row_a + 1))
    da = key_row - row_a + (NA_ROWS - 1)
    return da - 1 if (va and vb) else NA_ONLY_A + da if va else NA_ONLY_B - 1 + da if vb else NA_NEITHER


def _na_kernel(q_ref, k_ref, v_ref, kc_ref, vc_ref, tab_ref, o_ref, vt_ref, vct_ref):
    n_chunks = vt_ref.shape[0]
    chunk = lambda c: _rows(c, NA_CHUNK)
    band_start = lambda blk: min(max(blk - 1, 0), n_chunks - NA_BAND_CHUNKS)

    _transpose_to(vct_ref, vc_ref[...])

    def transpose_chunk(c, carry):
        _transpose_to(vt_ref.at[c], v_ref[chunk(c), :])
        return carry
    lax.fori_loop(0, n_chunks, transpose_chunk, 0)

    def scores(blk):
        row0 = blk * NA_QROWS
        cb = band_start(blk)
        q = q_ref[chunk(blk), :]
        s = []
        for c in range(NA_BAND_CHUNKS):
            tiles = []
            for i in range(NA_QROWS):
                key_row = (cb + c) * NA_QROWS + i
                tiles.append(jnp.concatenate(
                    [tab_ref[_na_tile_index(key_row, row0 + 2 * j)] for j in range(NA_QROWS // 2)], axis=1))
            s.append(_nt_dot(k_ref[chunk(cb + c), :], q) * SCALE_LOG2 + jnp.concatenate(tiles, axis=0))
        s.append(_nt_dot(kc_ref[...], q) * SCALE_LOG2)
        return s

    def probs(blk, s):
        m = functools.reduce(jnp.maximum, [jnp.max(x, axis=0, keepdims=True) for x in s])
        p = [jnp.exp2(x - m) for x in s]
        l = functools.reduce(jnp.add, [jnp.sum(x, axis=0, keepdims=True) for x in p])
        return [x.astype(BF16) for x in p], 1.0 / l

    def output(blk, p_linv):
        p, linv = p_linv
        cb = band_start(blk)
        o_t = jnp.dot(vct_ref[...], p[NA_BAND_CHUNKS], preferred_element_type=F32)
        for c in range(NA_BAND_CHUNKS):
            o_t = o_t + jnp.dot(vt_ref[cb + c], p[c], preferred_element_type=F32)
        o_ref[chunk(blk), :] = (o_t * linv).T.astype(o_ref.dtype)

    _pipeline3(n_chunks, scores, probs, output)


def neighbourhood_attention(proj_lat, proj_ctx, bias_table):
    assert CTX_LEN == NA_CHUNK
    lat = lambda base: pl.BlockSpec((None, SEQ, HEAD_DIM), lambda b, h: (base + h, b, 0))
    ctx = lambda base: pl.BlockSpec((None, CTX_LEN, HEAD_DIM), lambda b, h: (base + h, b, 0))
    return pl.pallas_call(
        _na_kernel,
        grid=(BATCH, C_HEADS),
        in_specs=[lat(0), lat(C_HEADS), lat(2 * C_HEADS), ctx(0), ctx(C_HEADS),
                  pl.BlockSpec((None, 3 * N_DR, GRID_W, 2 * GRID_W), lambda b, h: (h, 0, 0, 0))],
        out_specs=pl.BlockSpec((SEQ, HEAD_DIM), lambda b, h: (b, h)),
        out_shape=jax.ShapeDtypeStruct((BATCH * SEQ, C_HEADS * HEAD_DIM), BF16),
        scratch_shapes=[pltpu.VMEM((SEQ // NA_CHUNK, HEAD_DIM, NA_CHUNK), BF16),
                        pltpu.VMEM((HEAD_DIM, CTX_LEN), BF16)],
        compiler_params=_params(("arbitrary", "arbitrary")),
        name="neighbourhood_attention",
    )(proj_lat, proj_lat, proj_lat, proj_ctx, proj_ctx, bias_table)


def _rope_tables():
    t = jnp.arange(SEQ, dtype=jnp.int32)
    row = (t // GRID_W).astype(F32)
    col = (t % GRID_W).astype(F32)
    n_freq = HEAD_DIM // 4
    freqs = ROPE_THETA ** (-jnp.arange(n_freq, dtype=F32) / n_freq)
    ar, ac = row[:, None] * freqs, col[:, None] * freqs
    cos_t = jnp.concatenate([jnp.cos(ar), jnp.cos(ar), jnp.cos(ac), jnp.cos(ac)], axis=1)
    sin_t = jnp.concatenate([-jnp.sin(ar), jnp.sin(ar), -jnp.sin(ac), jnp.sin(ac)], axis=1)
    return cos_t, sin_t


def _na_bias_table(rel_bias):
    col = jnp.arange(GRID_W, dtype=jnp.int32)
    c0 = jnp.clip(col - NA_COLS // 2, 0, GRID_W - NA_COLS)
    key, qry = col[:, None], col[None, :]
    dc = key - qry + (NA_COLS - 1)
    valid = (key >= c0[None, :]) & (key < c0[None, :] + NA_COLS)
    t = rel_bias[:, :, jnp.clip(dc, 0, 2 * NA_COLS - 2)].astype(F32) * LOG2E
    t = jnp.where(valid[None, None], t, NEG)
    off = jnp.full_like(t, NEG)
    both = jnp.concatenate([t[:, 1:], t[:, :-1]], axis=-1)
    only_a = jnp.concatenate([t, off], axis=-1)
    only_b = jnp.concatenate([off, t], axis=-1)
    neither = jnp.concatenate([off[:, :1], off[:, :1]], axis=-1)
    return jnp.concatenate([both, only_a, only_b, neither], axis=1)


def kernel(x, c, ctx, c_ctx, ada_w, ada_b, norm1_g, norm2_g, w_in_even, w_out_even, a_q_norm, a_k_norm,
           b_lambda_q1, b_lambda_k1, b_lambda_q2, b_lambda_k2, b_subln_g, w_in_odd, w_out_odd, na_rel_bias,
           mlp_w1, mlp_w2, final_g):
    TM = 1024
    TC = BATCH * CTX_LEN
    TN = 512
    lat_row = lambda m: m // (SEQ // TM)
    lat_row_n = lambda m: m // (SEQ // TN)
    ctx_row = lambda m: 2

    craw = jnp.concatenate([c, c_ctx[None], jnp.zeros((5, D_MODEL), F32)], axis=0)
    mod = ada_modulation(craw, ada_w, ada_b).reshape(DEPTH, 8, 6, 1, D_MODEL)

    h = x.reshape(BATCH * SEQ, D_MODEL)
    hc = ctx.reshape(TC, D_MODEL)
    cos_t, sin_t = _rope_tables()
    one_t, zero_t = jnp.ones((TC, HEAD_DIM), F32), jnp.zeros((TC, HEAD_DIM), F32)

    mod0 = mod[0]
    w_in = cast_layer_bf16(w_in_even, 0)
    w_out = cast_layer_bf16(w_out_even, 0)
    ones = jnp.ones((HEAD_DIM,), F32)
    gains = jnp.stack([a_q_norm[0] * SCALE_LOG2] * A_HEADS + [a_k_norm[0]] * A_KV_HEADS
                      + [ones * SCALE_LOG2] * (2 * B_HEADS) + [ones] * (2 * B_HEADS)).reshape(N_ROT, 1, HEAD_DIM)
    lam_params = (b_lambda_q1[0], b_lambda_k1[0], b_lambda_q2[0], b_lambda_k2[0])
    lambda_init = 0.8 - 0.6 * math.exp(-0.3 * 0)

    u = norm_modulate(h, norm1_g[0], mod0, 0, 1, lat_row_n, TN)
    uc = norm_modulate(hc, norm1_g[0], mod0, 0, 1, ctx_row, TC)
    proj = project_heads(u, w_in, TM, 1024)
    proj_c = project_heads(uc, w_in, TC, 1024)
    rot = prep_qk(proj, cos_t, sin_t, gains, 2048)
    rot_c = prep_qk(proj_c, one_t, zero_t, gains, TC)

    a_lat = gqa_attention(rot, proj, rot, proj, rot_c, proj_c, 256, 1024)
    b_lat = diff_attention(rot, rot, proj, rot_c, proj_c, lam_params, b_subln_g[0], lambda_init, 512, 1024)
    a_ctx = gqa_attention(rot_c, proj_c, None, None, rot_c, proj_c, 128, 512)
    b_ctx = diff_attention(rot_c, None, None, rot_c, proj_c, lam_params, b_subln_g[0], lambda_init, 256, 512)

    h = out_project_residual([a_lat, b_lat], w_out, h, mod0, 2, lat_row, TM, 1024)
    hc = out_project_residual([a_ctx, b_ctx], w_out, hc, mod0, 2, ctx_row, TC, 1024)
    w1 = cast_layer_bf16(mlp_w1, 0)
    w2 = cast_layer_bf16(mlp_w2, 0)
    h = mlp_residual(h, norm2_g[0], mod0, lat_row, w1, w2, final_g, False, TM, 512)
    hc = mlp_residual(hc, norm2_g[0], mod0, ctx_row, w1, w2, final_g, False, TC, 512)

    mod1 = mod[1]
    w_in = cast_layer_bf16(w_in_odd, 0)
    u = norm_modulate(h, norm1_g[1], mod1, 0, 1, lat_row_n, TN)
    uc = norm_modulate(hc, norm1_g[1], mod1, 0, 1, ctx_row, TC)
    proj = project_heads(u, w_in, TM, 1024)
    proj_c = project_heads(uc, w_in, TC, 1024, col0=C_HEADS * HEAD_DIM)
    attn = neighbourhood_attention(proj, proj_c, _na_bias_table(na_rel_bias[0]))
    h = out_project_residual([attn], cast_layer_bf16(w_out_odd, 0), h, mod1, 2, lat_row, TM, 1024)
    out = mlp_residual(h, norm2_g[1], mod1, lat_row, cast_layer_bf16(mlp_w1, 1), cast_layer_bf16(mlp_w2, 1),
                       final_g, True, TM, 512)
    return out.reshape(BATCH, SEQ, D_MODEL)
```
